```python
import math
import jax, jax.numpy as jnp
from jax import lax
import numpy as np

D_MODEL = 1024
BATCH = 4
SEQ = 8192
DEPTH = 1
DEC_BATCH = 32
DEC_SEQ = 8
PAST_LEN = 16384
PAGE_SIZE = 128

HEAD_DIM = 64
ATTN_WIDTH = D_MODEL // 2
N_HEADS = ATTN_WIDTH // HEAD_DIM
N_KV_HEADS = 2
Q_PER_KV = N_HEADS // N_KV_HEADS
KV_WIDTH = N_KV_HEADS * HEAD_DIM
N_IDX_HEADS = 4
IDX_DIM = 64
TOPK_MAX = 256
Q_BLOCK = 128
ROPE_THETA = 10000.0
SSM_WIDTH = D_MODEL - ATTN_WIDTH
SSM_GROUP = 16
N_SSM_GROUPS = SSM_WIDTH // SSM_GROUP
SSM_STATE = 64
DT_MIN = 1e-3
DT_MAX = 1e-1
D_FF = 4 * D_MODEL
RMS_EPS = 1e-6
NEG = -1e30
OFF_Q = 0
OFF_K = OFF_Q + ATTN_WIDTH
OFF_V = OFF_K + KV_WIDTH
OFF_QI = OFF_V + KV_WIDTH
OFF_KI = OFF_QI + N_IDX_HEADS * IDX_DIM
OFF_WI = OFF_KI + IDX_DIM
OFF_U = OFF_WI + N_IDX_HEADS
IN_WIDTH = OFF_U + SSM_WIDTH

kernel_name = 'hymba_dsa_s5_decoder_step'


def rmsnorm(x, g):
    xf = x.astype(jnp.float32)
    y = xf * lax.rsqrt(jnp.mean(xf * xf, axis=-1, keepdims=True) + RMS_EPS)
    return (y * g.astype(jnp.float32)).astype(x.dtype)


def rope(x, pos):
    half = x.shape[-1] // 2
    inv = ROPE_THETA ** (-jnp.arange(half, dtype=jnp.float32) / half)
    ang = pos.astype(jnp.float32)[:, None] * inv[None, :]
    cos = jnp.cos(ang)[None, :, None, :]
    sin = jnp.sin(ang)[None, :, None, :]
    xf = x.astype(jnp.float32)
    x1, x2 = xf[..., :half], xf[..., half:]
    return jnp.concatenate([x1 * cos - x2 * sin, x1 * sin + x2 * cos], axis=-1).astype(x.dtype)


def mixer_inputs(x, pos, g_pre, w_in):
    B, T, _ = x.shape
    proj = rmsnorm(x, g_pre) @ w_in
    q = rope(proj[..., OFF_Q:OFF_K].reshape(B, T, N_HEADS, HEAD_DIM), pos)
    k = rope(proj[..., OFF_K:OFF_V].reshape(B, T, N_KV_HEADS, HEAD_DIM), pos)
    v = proj[..., OFF_V:OFF_QI].reshape(B, T, N_KV_HEADS, HEAD_DIM)
    qi = rope(proj[..., OFF_QI:OFF_KI].reshape(B, T, N_IDX_HEADS, IDX_DIM), pos)
    ki = rope(proj[..., OFF_KI:OFF_WI].reshape(B, T, 1, IDX_DIM), pos)[:, :, 0]
    wi = proj[..., OFF_WI:OFF_U]
    u = proj[..., OFF_U:]
    return q, k, v, qi, ki, wi, u


def indexer_topk(qi, wi, ki, q_pos, topk):
    s = jnp.einsum('bqhd,bld->bqhl', qi.astype(jnp.float32), ki.astype(jnp.float32)) * (IDX_DIM ** -0.5)
    score = jnp.einsum('bqhl,bqh->bql', jax.nn.relu(s), wi.astype(jnp.float32)) * (N_IDX_HEADS ** -0.5)
    key_pos = jnp.arange(ki.shape[1], dtype=jnp.int32)
    admissible = key_pos[None, :] <= q_pos[:, None]
    score = jnp.where(admissible[None], score, -jnp.inf)
    _, idx = lax.top_k(score, topk)
    valid = idx <= q_pos[None, :, None]
    return idx, valid


def sparse_attend(q, kg, vg, valid):
    B, Q = q.shape[:2]
    qg = q.reshape(B, Q, N_KV_HEADS, Q_PER_KV, HEAD_DIM).astype(jnp.float32)
    logits = jnp.einsum('bqngd,bqknd->bqngk', qg, kg.astype(jnp.float32)) * (HEAD_DIM ** -0.5)
    logits = jnp.where(valid[:, :, None, None, :], logits, NEG)
    p = jax.nn.softmax(logits, axis=-1)
    o = jnp.einsum('bqngk,bqknd->bqngd', p, vg.astype(jnp.float32))
    return o.reshape(B, Q, ATTN_WIDTH).astype(q.dtype)


def dsa_prompt(q, k, v, qi, ki, wi):
    B, T = q.shape[:2]
    nb = T // Q_BLOCK
    topk = min(TOPK_MAX, T // 4)
    take = jax.vmap(lambda rows, ids: rows[ids])

    def to_blocks(a):
        return jnp.moveaxis(a.reshape((B, nb, Q_BLOCK) + a.shape[2:]), 1, 0)

    def one_block(args):
        qb, qib, wib, start = args
        q_pos = start + jnp.arange(Q_BLOCK, dtype=jnp.int32)
        idx, valid = indexer_topk(qib, wib, ki, q_pos, topk)
        return sparse_attend(qb, take(k, idx), take(v, idx), valid)

    starts = jnp.arange(nb, dtype=jnp.int32) * Q_BLOCK
    out = lax.map(one_block, (to_blocks(q), to_blocks(qi), to_blocks(wi), starts))
    return jnp.moveaxis(out, 0, 1).reshape(B, T, ATTN_WIDTH)


def gather_paged(pool, new, idx, page_table):
    DB = idx.shape[0]
    n_pages = page_table.shape[1]
    past_len = n_pages * PAGE_SIZE
    page = jnp.minimum(idx // PAGE_SIZE, n_pages - 1)
    phys = jnp.take_along_axis(page_table, page.reshape(DB, -1), axis=1).reshape(idx.shape)
    past_rows = pool[phys, idx % PAGE_SIZE]
    new_rows = jax.vmap(lambda rows, ids: rows[ids])(new, jnp.clip(idx - past_len, 0, new.shape[1] - 1))
    in_past = (idx < past_len).reshape(idx.shape + (1,) * (past_rows.ndim - 3))
    return jnp.where(in_past, past_rows, new_rows)


def s5_scan(u, h0_re, h0_im, lam_re, lam_im, log_dt, b_re, b_im, c_re, c_im, d_skip):
    B, T, _ = u.shape
    uf = u.astype(jnp.float32).reshape(B, T, N_SSM_GROUPS, SSM_GROUP)
    dt = jnp.exp(log_dt.astype(jnp.float32))[:, None]
    lr, li = lam_re.astype(jnp.float32), lam_im.astype(jnp.float32)
    mag = jnp.exp(lr * dt)
    ar, ai = mag * jnp.cos(li * dt), mag * jnp.sin(li * dt)
    den = lr * lr + li * li
    cr = ((ar - 1.0) * lr + ai * li) / den
    ci = (ai * lr - (ar - 1.0) * li) / den
    br, bi = b_re.astype(jnp.float32), b_im.astype(jnp.float32)
    bbr = cr[..., None] * br - ci[..., None] * bi
    bbi = cr[..., None] * bi + ci[..., None] * br
    bu_re = jnp.einsum('gpc,btgc->btgp', bbr, uf)
    bu_im = jnp.einsum('gpc,btgc->btgp', bbi, uf)
    h0r, h0i = h0_re.astype(jnp.float32), h0_im.astype(jnp.float32)
    bu_re = bu_re.at[:, 0].add(ar * h0r - ai * h0i)
    bu_im = bu_im.at[:, 0].add(ar * h0i + ai * h0r)
    a_re = jnp.broadcast_to(ar, (1, T, N_SSM_GROUPS, SSM_STATE))
    a_im = jnp.broadcast_to(ai, (1, T, N_SSM_GROUPS, SSM_STATE))

    def combine(e1, e2):
        a1r, a1i, b1r, b1i = e1
        a2r, a2i, b2r, b2i = e2
        return (a1r * a2r - a1i * a2i, a1r * a2i + a1i * a2r,
                a2r * b1r - a2i * b1i + b2r, a2r * b1i + a2i * b1r + b2i)

    _, _, hr, hi = lax.associative_scan(combine, (a_re, a_im, bu_re, bu_im), axis=1)
    y = (jnp.einsum('gcp,btgp->btgc', c_re.astype(jnp.float32), hr)
         - jnp.einsum('gcp,btgp->btgc', c_im.astype(jnp.float32), hi)
         + d_skip.astype(jnp.float32).reshape(N_SSM_GROUPS, SSM_GROUP) * uf)
    return y.reshape(B, T, SSM_WIDTH), hr[:, -1], hi[:, -1]


def glu_out(y, w_glu, b_glu, dtype):
    z = jax.nn.gelu(y)
    gate = jax.nn.sigmoid(z @ w_glu.astype(jnp.float32) + b_glu.astype(jnp.float32))
    return (z * gate).astype(dtype)


def residual_block(x, a_o, s_o, w_out, g_post, g_ffn_pre, w_up, w_down, g_ffn_post):
    mix = jnp.concatenate([a_o, s_o], axis=-1) @ w_out
    x = x + rmsnorm(mix, g_post)
    hdn = jnp.square(jax.nn.relu(rmsnorm(x, g_ffn_pre) @ w_up))
    return x + rmsnorm(hdn @ w_down, g_ffn_post)


def setup_inputs(seed: int = 0) -> dict:
    key = jax.random.key(seed)
    ks = jax.random.split(key, 32)
    n_pages = PAST_LEN // PAGE_SIZE
    n_used = DEC_BATCH * n_pages
    n_phys = n_used + max(1, n_used // 4)
    nrm = jax.random.normal
    f32 = jnp.float32
    page_table = jax.random.permutation(ks[0], n_phys)[:n_used].reshape(DEC_BATCH, n_pages).astype(jnp.int32)
    log_dt = jax.random.uniform(ks[1], (DEPTH, N_SSM_GROUPS), f32, math.log(DT_MIN), math.log(DT_MAX))
    lam_im = (jnp.pi * jnp.arange(SSM_STATE, dtype=f32))[None, None, :] + 0.01 * nrm(ks[2], (DEPTH, N_SSM_GROUPS, SSM_STATE), f32)
    return {
        'x_prompt': nrm(ks[3], (BATCH, SEQ, D_MODEL), f32),
        'x_sample': nrm(ks[4], (DEC_BATCH, DEC_SEQ, D_MODEL), f32),
        'cache_k': nrm(ks[5], (DEPTH, n_phys, PAGE_SIZE, N_KV_HEADS, HEAD_DIM), f32),
        'cache_v': nrm(ks[6], (DEPTH, n_phys, PAGE_SIZE, N_KV_HEADS, HEAD_DIM), f32),
        'cache_kidx': nrm(ks[7], (DEPTH, n_phys, PAGE_SIZE, IDX_DIM), f32),
        'state_ssm_re': 0.5 * nrm(ks[8], (DEPTH, DEC_BATCH, N_SSM_GROUPS, SSM_STATE), f32),
        'state_ssm_im': 0.5 * nrm(ks[9], (DEPTH, DEC_BATCH, N_SSM_GROUPS, SSM_STATE), f32),
        'page_table': page_table,
        'norm_mix_pre': 1.0 + 0.05 * nrm(ks[10], (DEPTH, D_MODEL), f32),
        'w_in': nrm(ks[11], (DEPTH, D_MODEL, IN_WIDTH), f32) * D_MODEL ** -0.5,
        'lambda_re': -0.5 + 0.01 * nrm(ks[12], (DEPTH, N_SSM_GROUPS, SSM_STATE), f32),
        'lambda_im': lam_im,
        'log_dt': log_dt,
        'b_re': nrm(ks[13], (DEPTH, N_SSM_GROUPS, SSM_STATE, SSM_GROUP), f32) * (2 * SSM_GROUP) ** -0.5,
        'b_im': nrm(ks[14], (DEPTH, N_SSM_GROUPS, SSM_STATE, SSM_GROUP), f32) * (2 * SSM_GROUP) ** -0.5,
        'c_re': nrm(ks[15], (DEPTH, N_SSM_GROUPS, SSM_GROUP, SSM_STATE), f32) * (2 * SSM_STATE) ** -0.5,
        'c_im': nrm(ks[16], (DEPTH, N_SSM_GROUPS, SSM_GROUP, SSM_STATE), f32) * (2 * SSM_STATE) ** -0.5,
        'd_skip': nrm(ks[17], (DEPTH, SSM_WIDTH), f32),
        'w_glu': nrm(ks[18], (DEPTH, SSM_WIDTH, SSM_WIDTH), f32) * SSM_WIDTH ** -0.5,
        'b_glu': 0.01 * nrm(ks[19], (DEPTH, SSM_WIDTH), f32),
        'w_out': nrm(ks[20], (DEPTH, ATTN_WIDTH + SSM_WIDTH, D_MODEL), f32) * (ATTN_WIDTH + SSM_WIDTH) ** -0.5,
        'norm_mix_post': 1.0 + 0.05 * nrm(ks[21], (DEPTH, D_MODEL), f32),
        'norm_ffn_pre': 1.0 + 0.05 * nrm(ks[22], (DEPTH, D_MODEL), f32),
        'w_up': nrm(ks[23], (DEPTH, D_MODEL, D_FF), f32) * D_MODEL ** -0.5,
        'w_down': nrm(ks[24], (DEPTH, D_FF, D_MODEL), f32) * D_FF ** -0.5,
        'norm_ffn_post': 1.0 + 0.05 * nrm(ks[25], (DEPTH, D_MODEL), f32),
    }


def reference(x_prompt, x_sample, cache_k, cache_v, cache_kidx, state_ssm_re, state_ssm_im, page_table,
              norm_mix_pre, w_in, lambda_re, lambda_im, log_dt, b_re, b_im, c_re, c_im, d_skip,
              w_glu, b_glu, w_out, norm_mix_post, norm_ffn_pre, w_up, w_down, norm_ffn_post):
    past_len = page_table.shape[1] * PAGE_SIZE
    dec_b, dec_t = x_sample.shape[0], x_sample.shape[1]
    pos_p = jnp.arange(x_prompt.shape[1], dtype=jnp.int32)
    pos_s = past_len + jnp.arange(dec_t, dtype=jnp.int32)
    topk_s = min(TOPK_MAX, (past_len + dec_t) // 4)
    hp, hs = x_prompt, x_sample
    kp_l, vp_l, kip_l, srp_l, sip_l = [], [], [], [], []
    ks_l, vs_l, kis_l, srs_l, sis_l = [], [], [], [], []
    for l in range(DEPTH):
        ssm_w = (lambda_re[l], lambda_im[l], log_dt[l], b_re[l], b_im[l], c_re[l], c_im[l], d_skip[l])
        tail = (w_out[l], norm_mix_post[l], norm_ffn_pre[l], w_up[l], w_down[l], norm_ffn_post[l])
        q, k, v, qi, ki, wi, u = mixer_inputs(hp, pos_p, norm_mix_pre[l], w_in[l])
        a_o = dsa_prompt(q, k, v, qi, ki, wi)
        h0 = jnp.zeros((hp.shape[0], N_SSM_GROUPS, SSM_STATE), jnp.float32)
        y_ssm, hr, hi = s5_scan(u, h0, h0, *ssm_w)
        s_o = glu_out(y_ssm, w_glu[l], b_glu[l], hp.dtype)
        hp = residual_block(hp, a_o, s_o, *tail)
        kp_l.append(k); vp_l.append(v); kip_l.append(ki); srp_l.append(hr); sip_l.append(hi)
        q, k, v, qi, ki, wi, u = mixer_inputs(hs, pos_s, norm_mix_pre[l], w_in[l])
        ki_past = cache_kidx[l][page_table].reshape(dec_b, past_len, IDX_DIM)
        ki_all = jnp.concatenate([ki_past, ki.astype(ki_past.dtype)], axis=1)
        idx, valid = indexer_topk(qi, wi, ki_all, pos_s, topk_s)
        kg = gather_paged(cache_k[l], k, idx, page_table)
        vg = gather_paged(cache_v[l], v, idx, page_table)
        a_o = sparse_attend(q, kg, vg, valid)
        y_ssm, hr, hi = s5_scan(u, state_ssm_re[l], state_ssm_im[l], *ssm_w)
        s_o = glu_out(y_ssm, w_glu[l], b_glu[l], hs.dtype)
        hs = residual_block(hs, a_o, s_o, *tail)
        ks_l.append(k); vs_l.append(v); kis_l.append(ki); srs_l.append(hr); sis_l.append(hi)
    return (hp, hs,
            jnp.stack(kp_l), jnp.stack(vp_l), jnp.stack(kip_l), jnp.stack(srp_l), jnp.stack(sip_l),
            jnp.stack(ks_l), jnp.stack(vs_l), jnp.stack(kis_l), jnp.stack(srs_l), jnp.stack(sis_l))
```

```python
import functools
import math

import jax
import jax.numpy as jnp
from jax import lax
from jax.experimental import pallas as pl
from jax.experimental.pallas import tpu as pltpu

F32 = jnp.float32
BF16 = jnp.bfloat16
I32 = jnp.int32

D_MODEL = 1024
PAGE_SIZE = 128
HEAD_DIM = 64
ATTN_WIDTH = D_MODEL // 2
N_HEADS = ATTN_WIDTH // HEAD_DIM
N_KV_HEADS = 2
Q_PER_KV = N_HEADS // N_KV_HEADS
KV_WIDTH = N_KV_HEADS * HEAD_DIM
N_IDX_HEADS = 4
IDX_DIM = 64
TOPK_MAX = 256
ROPE_THETA = 10000.0
SSM_WIDTH = D_MODEL - ATTN_WIDTH
SSM_GROUP = 16
N_SSM_GROUPS = SSM_WIDTH // SSM_GROUP
SSM_STATE = 64
D_FF = 4 * D_MODEL
RMS_EPS = 1e-6
NEG = -1e30
OFF_Q = 0
OFF_K = OFF_Q + ATTN_WIDTH
OFF_V = OFF_K + KV_WIDTH
OFF_QI = OFF_V + KV_WIDTH
OFF_KI = OFF_QI + N_IDX_HEADS * IDX_DIM
OFF_WI = OFF_KI + IDX_DIM
OFF_U = OFF_WI + N_IDX_HEADS
IN_WIDTH = OFF_U + SSM_WIDTH

LANES = 128
SUBLANES = 8
INT_MIN = -(2 ** 31)
VMEM_LIMIT = 56 * 1024 * 1024

QPAD = N_HEADS * LANES
MAIN_W = QPAD + 2 * KV_WIDTH + SSM_WIDTH
IDX_W = 2 * N_IDX_HEADS * IDX_DIM + 2 * IDX_DIM + LANES
SSM_LANES = N_SSM_GROUPS * SSM_STATE
N_SLAB = 2 * SSM_LANES // LANES
N_UCHUNK = SSM_WIDTH // LANES
GROUPS_PER_CHUNK = LANES // SSM_GROUP
SLABS_PER_CHUNK = N_SLAB // N_UCHUNK

NT_DIMS = (((1,), (1,)), ((), ()))


def _cparams(n_grid):
    return pltpu.CompilerParams(dimension_semantics=("arbitrary",) * n_grid,
                                vmem_limit_bytes=VMEM_LIMIT)


def _const_spec(shape):
    zeros = (0,) * len(shape)
    return pl.BlockSpec(shape, lambda *_: zeros, pipeline_mode=pl.Buffered(1))


def _rms(x, g):
    return x * lax.rsqrt(jnp.mean(x * x, axis=-1, keepdims=True) + RMS_EPS) * g


def _split_bf16(x):
    hi = x.astype(BF16)
    lo = (x - hi.astype(F32)).astype(BF16)
    return hi, lo


def _sortable(x):
    b = lax.bitcast_convert_type(x, I32)
    return b ^ ((b >> 31) & jnp.int32(0x7FFFFFFF))


def _rope(x, cos, sin):
    w = x.shape[-1]
    lane = lax.broadcasted_iota(I32, x.shape, 1)
    first = (lane & (HEAD_DIM - 1)) < HEAD_DIM // 2
    sw = jnp.where(first, pltpu.roll(x, w - HEAD_DIM // 2, 1), pltpu.roll(x, HEAD_DIM // 2, 1))
    return x * cos + sw * sin


def _inproj_kernel(x_ref, g_ref, wm_ref, wih_ref, wil_ref, cos_ref, sin_ref,
                   q_ref, kf_ref, kb_ref, vf_ref, vb_ref, u_ref, kif_ref, ki4_ref, qi4_ref, misc_ref):
    xn = _rms(x_ref[...], g_ref[...])
    xh, xl = _split_bf16(xn)
    cos = cos_ref[...]
    sin = sin_ref[...]
    main = jnp.dot(xh, wm_ref[...], preferred_element_type=F32)
    wih = wih_ref[...]
    idx = (jnp.dot(xh, wih, preferred_element_type=F32) + jnp.dot(xl, wih, preferred_element_type=F32)
           + jnp.dot(xh, wil_ref[...], preferred_element_type=F32))

    scale = HEAD_DIM ** -0.5
    for h in range(N_HEADS):
        blk = main[:, h * LANES:(h + 1) * LANES]
        q_ref[:, h * LANES:(h + 1) * LANES] = (_rope(blk, cos, sin) * scale).astype(BF16)
    k = _rope(main[:, QPAD:QPAD + KV_WIDTH], cos, sin)
    kf_ref[...] = k
    kb_ref[...] = k.astype(BF16)
    v = main[:, QPAD + KV_WIDTH:QPAD + 2 * KV_WIDTH]
    vf_ref[...] = v
    vb_ref[...] = v.astype(BF16)
    u_ref[...] = main[:, QPAD + 2 * KV_WIDTH:]

    for h in range(N_IDX_HEADS):
        d = _rope(idx[:, h * LANES:(h + 1) * LANES], cos, sin)
        hi, lo = _split_bf16(d)
        qi4_ref[:, 2 * h * LANES:(2 * h + 1) * LANES] = hi
        qi4_ref[:, (2 * h + 1) * LANES:(2 * h + 2) * LANES] = lo
    off = N_IDX_HEADS * LANES
    d = _rope(idx[:, off:off + LANES], cos, sin)
    kif_ref[...] = d[:, :IDX_DIM]
    hi = d.astype(BF16).astype(F32)
    lane = lax.broadcasted_iota(I32, d.shape, 1)
    hl = jnp.where(lane < IDX_DIM, hi, d - hi).astype(BF16)
    ki4_ref[:, :LANES] = hl
    ki4_ref[:, LANES:] = hl
    misc_ref[...] = idx[:, off + LANES:]


def _inproj(x, pos, g_pre, wm, wih, wil, tm):
    B, T, _ = x.shape
    half = HEAD_DIM // 2
    inv = ROPE_THETA ** (-jnp.arange(half, dtype=F32) / half)
    ang = pos.astype(F32)[:, None] * inv[None, :]
    cos = jnp.tile(jnp.cos(ang), (1, LANES // half))
    s = jnp.sin(ang)
    sin = jnp.tile(jnp.concatenate([-s, s], axis=1), (1, LANES // HEAD_DIM))

    def row(w):
        return pl.BlockSpec((None, tm, w), lambda b, t: (b, t, 0))

    def tab():
        return pl.BlockSpec((tm, LANES), lambda b, t: (t, 0))

    out_shapes = (
        jax.ShapeDtypeStruct((B, T, QPAD), BF16),
        jax.ShapeDtypeStruct((B, T, KV_WIDTH), F32),
        jax.ShapeDtypeStruct((B, T, KV_WIDTH), BF16),
        jax.ShapeDtypeStruct((B, T, KV_WIDTH), F32),
        jax.ShapeDtypeStruct((B, T, KV_WIDTH), BF16),
        jax.ShapeDtypeStruct((T, B * SSM_WIDTH), F32),
        jax.ShapeDtypeStruct((B, T, IDX_DIM), F32),
        jax.ShapeDtypeStruct((B, T, 4 * IDX_DIM), BF16),
        jax.ShapeDtypeStruct((B, T, 4 * IDX_DIM * N_IDX_HEADS), BF16),
        jax.ShapeDtypeStruct((B, T, LANES), F32),
    )
    out_specs = (
        row(QPAD), row(KV_WIDTH), row(KV_WIDTH), row(KV_WIDTH), row(KV_WIDTH),
        pl.BlockSpec((tm, SSM_WIDTH), lambda b, t: (t, b)),
        row(IDX_DIM), row(4 * IDX_DIM), row(4 * IDX_DIM * N_IDX_HEADS), row(LANES),
    )
    return pl.pallas_call(
        _inproj_kernel,
        grid=(B, T // tm),
        in_specs=[row(D_MODEL), _const_spec((1, D_MODEL)), _const_spec((D_MODEL, MAIN_W)),
                  _const_spec((D_MODEL, IDX_W)), _const_spec((D_MODEL, IDX_W)), tab(), tab()],
        out_specs=out_specs,
        out_shape=out_shapes,
        compiler_params=_cparams(2),
        name="inproj",
    )(x, g_pre.reshape(1, D_MODEL), wm, wih, wil, cos, sin)


def _inproj_weights(w_in):
    cols = []
    zero = jnp.zeros((D_MODEL, HEAD_DIM), F32)
    for h in range(N_HEADS):
        wq = w_in[:, OFF_Q + h * HEAD_DIM:OFF_Q + (h + 1) * HEAD_DIM]
        cols += [wq, zero] if h // Q_PER_KV == 0 else [zero, wq]
    cols += [w_in[:, OFF_K:OFF_QI], w_in[:, OFF_U:]]
    wm = jnp.concatenate(cols, axis=1).astype(BF16)
    icols = []
    for h in range(N_IDX_HEADS):
        wq = w_in[:, OFF_QI + h * IDX_DIM:OFF_QI + (h + 1) * IDX_DIM]
        icols += [wq, wq]
    wk = w_in[:, OFF_KI:OFF_WI]
    icols += [wk, wk, w_in[:, OFF_WI:OFF_U], jnp.zeros((D_MODEL, LANES - N_IDX_HEADS), F32)]
    wi = jnp.concatenate(icols, axis=1)
    wih = wi.astype(BF16)
    wil = (wi - wih.astype(F32)).astype(BF16)
    return wm, wih, wil


def _attn_kernel(qi4_ref, wit_ref, ki4_ref, q_ref, kb_ref, vb_ref, o_ref,
                 keys_ref, qs_ref, m_ref, l_ref, acc_ref, *, topk, tq):
    i = pl.program_id(1)
    nk = i + 1
    w = wit_ref[...]

    def chunk(c):
        return pl.ds(pl.multiple_of(c * tq, tq), tq)

    def score_body(c, _):
        kc = ki4_ref[chunk(c), :]
        sc = None
        for h in range(N_IDX_HEADS):
            s = lax.dot_general(kc, qi4_ref[:, 4 * IDX_DIM * h:4 * IDX_DIM * (h + 1)], NT_DIMS,
                                preferred_element_type=F32)
            t = jnp.maximum(s, 0.0) * w[h:h + 1, :]
            sc = t if sc is None else sc + t
        key = _sortable(sc)
        kpos = c * tq + lax.broadcasted_iota(I32, key.shape, 0)
        qpos = i * tq + lax.broadcasted_iota(I32, key.shape, 1)
        keys_ref[chunk(c), :] = jnp.where(kpos <= qpos, key, INT_MIN)
        return 0

    lax.fori_loop(0, nk, score_body, 0)

    def count(pred):
        def body(c, acc):
            m = jnp.where(pred(keys_ref[chunk(c), :]), 1, 0).astype(I32)
            return acc + jnp.sum(m.reshape(tq // SUBLANES, SUBLANES, tq), axis=0)
        acc = lax.fori_loop(0, nk, body, jnp.zeros((SUBLANES, tq), I32))
        return jnp.sum(acc, axis=0, keepdims=True)

    zero = jnp.zeros((1, tq), I32)
    tau = jnp.where(count(lambda k: k >= zero) >= topk, 0, INT_MIN).astype(I32)

    def bit_body(b, tau):
        cand = tau + (jnp.int32(1) << (30 - b))
        return jnp.where(count(lambda k: k >= cand) >= topk, cand, tau)

    tau = lax.fori_loop(0, 31, bit_body, tau)
    tau = jnp.maximum(tau, INT_MIN + 1)
    quota = (topk - count(lambda k: k > tau)).astype(F32)

    for h in range(N_HEADS):
        qs_ref[h * tq:(h + 1) * tq, :] = q_ref[:, h * LANES:(h + 1) * LANES]
    m_ref[...] = jnp.full(m_ref.shape, NEG, F32)
    l_ref[...] = jnp.zeros(l_ref.shape, F32)
    acc_ref[...] = jnp.zeros(acc_ref.shape, F32)
    r_i = lax.broadcasted_iota(I32, (tq, tq), 0)
    c_i = lax.broadcasted_iota(I32, (tq, tq), 1)
    ltri = jnp.where(c_i <= r_i, 1.0, 0.0).astype(BF16)

    def attn_body(c, carry):
        kc = keys_ref[chunk(c), :]
        gt = kc > tau
        eq = kc == tau
        e = jnp.where(eq, 1.0, 0.0)
        pre = jnp.dot(ltri, e.astype(BF16), preferred_element_type=F32)
        rank = carry + pre - e
        sel_t = jnp.where(gt | (eq & (rank < quota)), 1.0, 0.0)
        carry = carry + pre[tq - 1:tq, :]
        sel = (sel_t.T > 0.5)[None]
        lg = lax.dot_general(qs_ref[...], kb_ref[chunk(c), :], NT_DIMS, preferred_element_type=F32)
        lg = jnp.where(sel, lg.reshape(N_HEADS, tq, tq), NEG).reshape(N_HEADS * tq, tq)
        m_prev = m_ref[...]
        m_new = jnp.maximum(m_prev, jnp.max(lg, axis=1, keepdims=True))
        alpha = jnp.exp(m_prev - m_new)
        p = jnp.exp(lg - m_new)
        l_ref[...] = alpha * l_ref[...] + jnp.sum(p, axis=1, keepdims=True)
        acc_ref[...] = alpha * acc_ref[...] + jnp.dot(p.astype(BF16), vb_ref[chunk(c), :],
                                                      preferred_element_type=F32)
        m_ref[...] = m_new
        return carry

    lax.fori_loop(0, nk, attn_body, jnp.zeros((1, tq), F32))
    out = acc_ref[...] / l_ref[...]
    for h in range(N_HEADS):
        o_ref[:, h * LANES:(h + 1) * LANES] = out[h * tq:(h + 1) * tq, :].astype(BF16)


def _prompt_attention(qi4, wit, ki4, q, kb, vb, tq):
    B, T, _ = q.shape
    topk = min(TOPK_MAX, T // 4)

    def tile(w):
        return pl.BlockSpec((None, tq, w), lambda b, i: (b, i, 0))

    def full(w):
        return pl.BlockSpec((None, T, w), lambda b, i: (b, 0, 0))

    return pl.pallas_call(
        functools.partial(_attn_kernel, topk=topk, tq=tq),
        grid=(B, T // tq),
        in_specs=[tile(4 * IDX_DIM * N_IDX_HEADS),
                  pl.BlockSpec((None, N_IDX_HEADS, tq), lambda b, i: (b, 0, i)),
                  full(4 * IDX_DIM), tile(QPAD), full(KV_WIDTH), full(KV_WIDTH)],
        out_specs=tile(QPAD),
        out_shape=jax.ShapeDtypeStruct((B, T, QPAD), BF16),
        scratch_shapes=[pltpu.VMEM((T, tq), I32), pltpu.VMEM((N_HEADS * tq, LANES), BF16),
                        pltpu.VMEM((N_HEADS * tq, 1), F32), pltpu.VMEM((N_HEADS * tq, 1), F32),
                        pltpu.VMEM((N_HEADS * tq, LANES), F32)],
        compiler_params=_cparams(2),
        name="prompt_attention",
    )(qi4, wit, ki4, q, kb, vb)


def _s5_prep_kernel(lr_ref, li_ref, ldt_ref, brt_ref, bit_ref, ar_ref, ai_ref, bbr_ref, bbi_ref):
    lr = lr_ref[...]
    li = li_ref[...]
    dt = jnp.exp(ldt_ref[...])
    mag = jnp.exp(lr * dt)
    ar = mag * jnp.cos(li * dt)
    ai = mag * jnp.sin(li * dt)
    den = lr * lr + li * li
    cr = ((ar - 1.0) * lr + ai * li) / den
    ci = (ai * lr - (ar - 1.0) * li) / den
    ar_ref[...] = ar
    ai_ref[...] = ai
    brt = brt_ref[...]
    bit = bit_ref[...]
    bbr_ref[...] = cr[:, None, :] * brt - ci[:, None, :] * bit
    bbi_ref[...] = cr[:, None, :] * bit + ci[:, None, :] * brt


def _s5_weights(lambda_re, lambda_im, log_dt, b_re, b_im, c_re, c_im):
    G, P, C = N_SSM_GROUPS, SSM_STATE, SSM_GROUP
    ar, ai, bbr, bbi = pl.pallas_call(
        _s5_prep_kernel,
        out_shape=(jax.ShapeDtypeStruct((G, P), F32), jax.ShapeDtypeStruct((G, P), F32),
                   jax.ShapeDtypeStruct((G, C, P), F32), jax.ShapeDtypeStruct((G, C, P), F32)),
        name="s5_prep",
    )(lambda_re, lambda_im, log_dt.reshape(G, 1), jnp.swapaxes(b_re, 1, 2), jnp.swapaxes(b_im, 1, 2))
    eye = jnp.eye(GROUPS_PER_CHUNK, dtype=F32)

    def in_blocks(bb):
        x = bb.reshape(N_UCHUNK, GROUPS_PER_CHUNK, C, P)
        x = x[:, :, :, None, :] * eye[None, :, None, :, None]
        return x.reshape(N_UCHUNK, LANES, GROUPS_PER_CHUNK * P)

    def out_blocks(cc):
        x = jnp.swapaxes(cc, 1, 2).reshape(N_UCHUNK, GROUPS_PER_CHUNK, P, C)
        x = x[:, :, :, None, :] * eye[None, :, None, :, None]
        return x.reshape(N_UCHUNK, GROUPS_PER_CHUNK * P, LANES)

    bblk = jnp.concatenate([in_blocks(bbr), in_blocks(bbi)], axis=2).astype(BF16)
    cblk = jnp.concatenate([out_blocks(c_re), out_blocks(-c_im)], axis=1).astype(BF16)
    half = SLABS_PER_CHUNK // 2

    def slabs(a):
        x = a.reshape(N_UCHUNK, 1, half, 1, LANES)
        return jnp.broadcast_to(x, (N_UCHUNK, 2, half, 1, LANES)).reshape(N_SLAB, 1, LANES)

    return slabs(ar), slabs(ai), bblk, cblk


def _state_to_slabs(h_re, h_im):
    B = h_re.shape[0]
    half = SLABS_PER_CHUNK // 2
    re = h_re.reshape(B, N_UCHUNK, 1, half, LANES)
    im = h_im.reshape(B, N_UCHUNK, 1, half, LANES)
    x = jnp.concatenate([re, im], axis=2).reshape(B, N_SLAB, LANES)
    return jnp.swapaxes(x, 0, 1)


def _slabs_to_state(h):
    B = h.shape[1]
    half = SLABS_PER_CHUNK // 2
    x = jnp.swapaxes(h, 0, 1).reshape(B, N_UCHUNK, 2, half * LANES)
    re = x[:, :, 0].reshape(B, N_SSM_GROUPS, SSM_STATE)
    im = x[:, :, 1].reshape(B, N_SSM_GROUPS, SSM_STATE)
    return re, im


def _s5_kernel(u_ref, h0_ref, ar_ref, ai_ref, bblk_ref, cblk_ref, d_ref, y_ref, hout_ref, hb_ref,
               *, nb, tj, base):
    rows = tj * nb
    half = SLABS_PER_CHUNK // 2

    @pl.when(pl.program_id(0) == 0)
    def _():
        hb_ref[:, base - nb:base, :] = h0_ref[...]

    u = u_ref[...]
    for q in range(N_UCHUNK):
        bu = jnp.dot(u[:, q * LANES:(q + 1) * LANES].astype(BF16), bblk_ref[q], preferred_element_type=F32)
        for s in range(SLABS_PER_CHUNK):
            hb_ref[q * SLABS_PER_CHUNK + s, base:base + rows, :] = bu[:, s * LANES:(s + 1) * LANES]

    def step(j, _):
        prev = pl.ds(base + (j - 1) * nb, nb)
        cur = pl.ds(base + j * nb, nb)
        for q in range(N_UCHUNK):
            for r in range(half):
                sr = q * SLABS_PER_CHUNK + r
                si = sr + half
                a_re = ar_ref[sr]
                a_im = ai_ref[sr]
                p_re = hb_ref[sr, prev, :]
                p_im = hb_ref[si, prev, :]
                hb_ref[sr, cur, :] = a_re * p_re - a_im * p_im + hb_ref[sr, cur, :]
                hb_ref[si, cur, :] = a_re * p_im + a_im * p_re + hb_ref[si, cur, :]
        return 0

    lax.fori_loop(0, tj, step, 0)

    for q in range(N_UCHUNK):
        hq = jnp.concatenate([hb_ref[q * SLABS_PER_CHUNK + s, base:base + rows, :]
                              for s in range(SLABS_PER_CHUNK)], axis=1).astype(BF16)
        yq = jnp.dot(hq, cblk_ref[q], preferred_element_type=F32)
        cols = slice(q * LANES, (q + 1) * LANES)
        y_ref[:, cols] = yq + d_ref[:, cols] * u[:, cols]
    last = hb_ref[:, base + rows - nb:base + rows, :]
    hout_ref[...] = last
    hb_ref[:, base - nb:base, :] = last


def _s5_scan(u_tb, h0_slabs, weights, d_skip, tj):
    T, nb, _ = u_tb.shape
    ar, ai, bblk, cblk = weights
    rows = tj * nb
    base = max(SUBLANES, nb)
    y, hout = pl.pallas_call(
        functools.partial(_s5_kernel, nb=nb, tj=tj, base=base),
        grid=(T // tj,),
        in_specs=[pl.BlockSpec((rows, SSM_WIDTH), lambda t: (t, 0)),
                  _const_spec((N_SLAB, nb, LANES)), _const_spec((N_SLAB, 1, LANES)),
                  _const_spec((N_SLAB, 1, LANES)),
                  _const_spec((N_UCHUNK, LANES, SLABS_PER_CHUNK * LANES)),
                  _const_spec((N_UCHUNK, SLABS_PER_CHUNK * LANES, LANES)),
                  _const_spec((1, SSM_WIDTH))],
        out_specs=(pl.BlockSpec((rows, SSM_WIDTH), lambda t: (t, 0)),
                   pl.BlockSpec((N_SLAB, nb, LANES), lambda t: (0, 0, 0))),
        out_shape=(jax.ShapeDtypeStruct((T * nb, SSM_WIDTH), F32),
                   jax.ShapeDtypeStruct((N_SLAB, nb, LANES), F32)),
        scratch_shapes=[pltpu.VMEM((N_SLAB, base + rows, LANES), F32)],
        compiler_params=_cparams(1),
        name="s5_scan",
    )(u_tb.reshape(T * nb, SSM_WIDTH), h0_slabs, ar, ai, bblk, cblk, d_skip.reshape(1, SSM_WIDTH))
    return y.reshape(T, nb, SSM_WIDTH), hout


def _post_kernel(x_ref, ao_ref, y_ref, wglu_ref, bglu_ref, woa_ref, wos_ref, gpost_ref, gpre_ref,
                 wup_ref, wdn_ref, gfpost_ref, o_ref):
    z = jax.nn.gelu(y_ref[...])
    gate = jax.nn.sigmoid(jnp.dot(z.astype(BF16), wglu_ref[...], preferred_element_type=F32) + bglu_ref[...])
    so = z * gate
    mix = (jnp.dot(ao_ref[...], woa_ref[...], preferred_element_type=F32)
           + jnp.dot(so.astype(BF16), wos_ref[...], preferred_element_type=F32))
    x = x_ref[...] + _rms(mix, gpost_ref[...])
    hdn = jnp.dot(_rms(x, gpre_ref[...]).astype(BF16), wup_ref[...], preferred_element_type=F32)
    hdn = jnp.square(jnp.maximum(hdn, 0.0))
    dn = jnp.dot(hdn.astype(BF16), wdn_ref[...], preferred_element_type=F32)
    o_ref[...] = x + _rms(dn, gfpost_ref[...])


def _post_weights(w_glu, w_out, w_up, w_down):
    zero = jnp.zeros((HEAD_DIM, D_MODEL), F32)
    rows = []
    for h in range(N_HEADS):
        wh = w_out[h * HEAD_DIM:(h + 1) * HEAD_DIM]
        rows += [wh, zero] if h // Q_PER_KV == 0 else [zero, wh]
    woa = jnp.concatenate(rows, axis=0).astype(BF16)
    return (w_glu.astype(BF16), woa, w_out[ATTN_WIDTH:].astype(BF16), w_up.astype(BF16), w_down.astype(BF16))


def _post(x, ao, y_tb, weights, b_glu, g_post, g_ffn_pre, g_ffn_post, tm):
    B, T, _ = x.shape
    wglu, woa, wos, wup, wdn = weights

    def row(w):
        return pl.BlockSpec((None, tm, w), lambda b, t: (b, t, 0))

    def vec(a):
        return a.reshape(1, -1)

    return pl.pallas_call(
        _post_kernel,
        grid=(B, T // tm),
        in_specs=[row(D_MODEL), row(QPAD), pl.BlockSpec((tm, SSM_WIDTH), lambda b, t: (t, b)),
                  _const_spec((SSM_WIDTH, SSM_WIDTH)), _const_spec((1, SSM_WIDTH)),
                  _const_spec((QPAD, D_MODEL)), _const_spec((SSM_WIDTH, D_MODEL)),
                  _const_spec((1, D_MODEL)), _const_spec((1, D_MODEL)),
                  _const_spec((D_MODEL, D_FF)), _const_spec((D_FF, D_MODEL)), _const_spec((1, D_MODEL))],
        out_specs=row(D_MODEL),
        out_shape=jax.ShapeDtypeStruct((B, T, D_MODEL), F32),
        compiler_params=_cparams(2),
        name="post",
    )(x, ao, y_tb.reshape(T, B * SSM_WIDTH), wglu, vec(b_glu), woa, wos, vec(g_post), vec(g_ffn_pre), wup, wdn,
      vec(g_ffn_post))


def _page_map(b, j, pt, *, r, n):
    return (pt[b, j * n + r], 0, 0)


def _sel_kernel(pt_ref, qh_ref, ql_ref, w_ref, kn_ref, *rest, topk, n_pages_step, past_len, dec_t):
    page_refs = rest[:n_pages_step]
    sel_ref = rest[n_pages_step]
    keys_ref = rest[n_pages_step + 1]
    j = pl.program_id(1)
    step_keys = n_pages_step * PAGE_SIZE
    qh = qh_ref[...]
    ql = ql_ref[...]
    w = w_ref[...]

    def scores(kc):
        kh, kl = _split_bf16(kc)
        s = (lax.dot_general(qh, kh, NT_DIMS, preferred_element_type=F32)
             + lax.dot_general(ql, kh, NT_DIMS, preferred_element_type=F32)
             + lax.dot_general(qh, kl, NT_DIMS, preferred_element_type=F32))
        sc = None
        for h in range(N_IDX_HEADS):
            t = jnp.maximum(s[h * dec_t:(h + 1) * dec_t], 0.0) * w[:, h:h + 1]
            sc = t if sc is None else sc + t
        return _sortable(sc)

    kc = jnp.concatenate([r[...] for r in page_refs], axis=0)
    keys_ref[:, pl.ds(pl.multiple_of(j * step_keys, step_keys), step_keys)] = scores(kc)

    @pl.when(j == pl.num_programs(1) - 1)
    def _():
        kn = scores(kn_ref[...])
        qi_ = lax.broadcasted_iota(I32, kn.shape, 0)
        ki_ = lax.broadcasted_iota(I32, kn.shape, 1)
        keys_ref[:, past_len:] = jnp.where(ki_ <= qi_, kn, INT_MIN)
        keys = keys_ref[...]

        def count(m):
            return jnp.sum(jnp.where(m, 1, 0).astype(I32), axis=1, keepdims=True)

        tau = jnp.where(count(keys >= 0) >= topk, 0, INT_MIN).astype(I32)

        def bit_body(b, tau):
            cand = tau + (jnp.int32(1) << (30 - b))
            return jnp.where(count(keys >= cand) >= topk, cand, tau)

        tau = lax.fori_loop(0, 31, bit_body, tau)
        tau = jnp.maximum(tau, INT_MIN + 1)
        gt = keys > tau
        eq = keys == tau
        quota = topk - count(gt)
        pos = lax.broadcasted_iota(I32, keys.shape, 1)
        n_bits = max(1, (keys.shape[1] - 1).bit_length())

        def idx_body(b, lim):
            cand = lim + (jnp.int32(1) << (n_bits - 1 - b))
            return jnp.where(count(eq & (pos < cand)) < quota, cand, lim)

        lim = lax.fori_loop(0, n_bits, idx_body, jnp.zeros_like(tau))
        sel_ref[...] = jnp.where(gt | (eq & (pos <= lim)), 1.0, 0.0)


def _sample_select(page_table, qh, ql, w, ki_new_pad, cache_kidx, n_pages_step):
    DB, n_pages = page_table.shape
    dec_t = w.shape[1]
    past_len = n_pages * PAGE_SIZE
    lp = past_len + LANES
    topk = min(TOPK_MAX, (past_len + dec_t) // 4)
    rows = N_IDX_HEADS * dec_t

    def fixed(shape):
        return pl.BlockSpec((None,) + shape, lambda b, j, pt: (b, 0, 0))

    pages = [pl.BlockSpec((None, PAGE_SIZE, IDX_DIM), functools.partial(_page_map, r=r, n=n_pages_step))
             for r in range(n_pages_step)]
    grid_spec = pltpu.PrefetchScalarGridSpec(
        num_scalar_prefetch=1,
        grid=(DB, n_pages // n_pages_step),
        in_specs=[fixed((rows, IDX_DIM)), fixed((rows, IDX_DIM)), fixed((dec_t, N_IDX_HEADS)),
                  fixed((LANES, IDX_DIM))] + pages,
        out_specs=fixed((dec_t, lp)),
        scratch_shapes=[pltpu.VMEM((dec_t, lp), I32)],
    )
    return pl.pallas_call(
        functools.partial(_sel_kernel, topk=topk, n_pages_step=n_pages_step, past_len=past_len, dec_t=dec_t),
        grid_spec=grid_spec,
        out_shape=jax.ShapeDtypeStruct((DB, dec_t, lp), F32),
        compiler_params=_cparams(2),
        name="sample_select",
    )(page_table, qh, ql, w, ki_new_pad, *([cache_kidx] * n_pages_step))


def _sattn_kernel(pt_ref, sel_ref, q_ref, kn_ref, vn_ref, *rest, n_pages_step, past_len, dec_t):
    k_refs = rest[:n_pages_step]
    v_refs = rest[n_pages_step:2 * n_pages_step]
    o_ref, m_ref, l_ref, acc_ref = rest[2 * n_pages_step:]
    j = pl.program_id(1)
    step_keys = n_pages_step * PAGE_SIZE
    q = q_ref[...]

    @pl.when(j == 0)
    def _():
        m_ref[...] = jnp.full(m_ref.shape, NEG, F32)
        l_ref[...] = jnp.zeros(l_ref.shape, F32)
        acc_ref[...] = jnp.zeros(acc_ref.shape, F32)

    def update(kc, vc, sel):
        n = kc.shape[0]
        lg = lax.dot_general(q, kc, NT_DIMS, preferred_element_type=F32)
        lg = jnp.where(sel[None] > 0.5, lg.reshape(N_HEADS, dec_t, n), NEG).reshape(N_HEADS * dec_t, n)
        m_prev = m_ref[...]
        m_new = jnp.maximum(m_prev, jnp.max(lg, axis=1, keepdims=True))
        alpha = jnp.exp(m_prev - m_new)
        p = jnp.exp(lg - m_new)
        l_ref[...] = alpha * l_ref[...] + jnp.sum(p, axis=1, keepdims=True)
        acc_ref[...] = alpha * acc_ref[...] + jnp.dot(p.astype(BF16), vc, preferred_element_type=F32)
        m_ref[...] = m_new

    kc = jnp.concatenate([r[...] for r in k_refs], axis=0).astype(BF16)
    vc = jnp.concatenate([r[...] for r in v_refs], axis=0).astype(BF16)
    update(kc, vc, sel_ref[:, pl.ds(pl.multiple_of(j * step_keys, step_keys), step_keys)])

    @pl.when(j == pl.num_programs(1) - 1)
    def _():
        update(kn_ref[...], vn_ref[...], sel_ref[:, past_len:])
        o_ref[...] = acc_ref[...] / l_ref[...]


def _sample_attention(page_table, sel, q, k_new_pad, v_new_pad, cache_k, cache_v, n_pages_step):
    DB, n_pages = page_table.shape
    dec_t = sel.shape[1]
    past_len = n_pages * PAGE_SIZE
    lp = past_len + LANES
    rows = N_HEADS * dec_t

    def fixed(shape):
        return pl.BlockSpec((None,) + shape, lambda b, j, pt: (b, 0, 0))

    pages = [pl.BlockSpec((None, PAGE_SIZE, KV_WIDTH), functools.partial(_page_map, r=r, n=n_pages_step))
             for r in range(n_pages_step)]
    grid_spec = pltpu.PrefetchScalarGridSpec(
        num_scalar_prefetch=1,
        grid=(DB, n_pages // n_pages_step),
        in_specs=[fixed((dec_t, lp)), fixed((rows, KV_WIDTH)), fixed((LANES, KV_WIDTH)),
                  fixed((LANES, KV_WIDTH))] + pages + pages,
        out_specs=fixed((rows, KV_WIDTH)),
        scratch_shapes=[pltpu.VMEM((rows, 1), F32), pltpu.VMEM((rows, 1), F32),
                        pltpu.VMEM((rows, KV_WIDTH), F32)],
    )
    return pl.pallas_call(
        functools.partial(_sattn_kernel, n_pages_step=n_pages_step, past_len=past_len, dec_t=dec_t),
        grid_spec=grid_spec,
        out_shape=jax.ShapeDtypeStruct((DB, rows, KV_WIDTH), F32),
        compiler_params=_cparams(2),
        name="sample_attention",
    )(page_table, sel, q, k_new_pad, v_new_pad, *([cache_k] * n_pages_step), *([cache_v] * n_pages_step))


def _largest_tile(n, cap):
    t = min(n, cap)
    while n % t:
        t //= 2
    return t


def _layer(hp, hs, cache_k, cache_v, cache_kidx, st_re, st_im, page_table, g_pre, w_in, ssm_params, d_skip,
           w_glu, b_glu, w_out, g_post, g_ffn_pre, w_up, w_down, g_ffn_post):
    B, T, _ = hp.shape
    DB, dec_t, _ = hs.shape
    n_pages = page_table.shape[1]
    past_len = n_pages * PAGE_SIZE
    idx_scale = (IDX_DIM ** -0.5) * (N_IDX_HEADS ** -0.5)

    wm, wih, wil = _inproj_weights(w_in)
    s5w = _s5_weights(*ssm_params)
    postw = _post_weights(w_glu, w_out, w_up, w_down)

    pos_p = jnp.arange(T, dtype=I32)
    q, kf, kb, vf, vb, u_tb, kif, ki4, qi4, misc = _inproj(hp, pos_p, g_pre, wm, wih, wil, _largest_tile(T, 512))
    wit = jnp.swapaxes(misc[:, :, :N_IDX_HEADS], 1, 2) * idx_scale
    ao = _prompt_attention(qi4, wit, ki4, q, kb, vb, _largest_tile(T, 256))
    h0 = jnp.zeros((N_SLAB, B, LANES), F32)
    y_tb, hfin = _s5_scan(u_tb.reshape(T, B, SSM_WIDTH), h0, s5w, d_skip, _largest_tile(T, 128))
    yp = _post(hp, ao, y_tb, postw, b_glu, g_post, g_ffn_pre, g_ffn_post, _largest_tile(T, 256))
    srp, sip = _slabs_to_state(hfin)
    prompt_out = (yp, kf.reshape(B, T, N_KV_HEADS, HEAD_DIM), vf.reshape(B, T, N_KV_HEADS, HEAD_DIM), kif, srp, sip)

    n_s = DB * dec_t
    pos_s = jnp.tile(past_len + jnp.arange(dec_t, dtype=I32), DB)
    q, kf, kb, vf, vb, u_s, kif, ki4, qi4, misc = _inproj(hs.reshape(1, n_s, D_MODEL), pos_s, g_pre, wm, wih, wil,
                                                          _largest_tile(n_s, 256))
    qi4 = qi4.reshape(DB, dec_t, N_IDX_HEADS, 4 * IDX_DIM)

    def heads_first(a):
        return jnp.swapaxes(a, 1, 2).reshape(DB, a.shape[2] * dec_t, a.shape[3])

    qh = heads_first(qi4[..., :IDX_DIM])
    ql = heads_first(qi4[..., 2 * IDX_DIM:3 * IDX_DIM])
    w_s = misc.reshape(DB, dec_t, LANES)[:, :, :N_IDX_HEADS] * idx_scale

    def pad_new(a):
        a = a.reshape(DB, dec_t, a.shape[-1])
        return jnp.pad(a, ((0, 0), (0, LANES - dec_t), (0, 0)))

    n_step = _largest_tile(n_pages, 8)
    sel = _sample_select(page_table, qh, ql, w_s, pad_new(kif), cache_kidx, n_step)
    qs = heads_first(q.reshape(DB, dec_t, N_HEADS, LANES))
    ao_s = _sample_attention(page_table, sel, qs, pad_new(kb), pad_new(vb),
                             cache_k.reshape(-1, PAGE_SIZE, KV_WIDTH), cache_v.reshape(-1, PAGE_SIZE, KV_WIDTH), n_step)
    ao_s = jnp.swapaxes(ao_s.reshape(DB, N_HEADS, dec_t, LANES), 1, 2).reshape(1, n_s, QPAD).astype(BF16)
    u_t = jnp.swapaxes(u_s.reshape(DB, dec_t, SSM_WIDTH), 0, 1)
    y_t, hfin = _s5_scan(u_t, _state_to_slabs(st_re, st_im), s5w, d_skip, dec_t)
    y_s = jnp.swapaxes(y_t, 0, 1).reshape(n_s, 1, SSM_WIDTH)
    ys = _post(hs.reshape(1, n_s, D_MODEL), ao_s, y_s, postw, b_glu, g_post, g_ffn_pre, g_ffn_post,
               _largest_tile(n_s, 256))
    srs, sis = _slabs_to_state(hfin)
    sample_out = (ys.reshape(DB, dec_t, D_MODEL), kf.reshape(DB, dec_t, N_KV_HEADS, HEAD_DIM),
                  vf.reshape(DB, dec_t, N_KV_HEADS, HEAD_DIM), kif.reshape(DB, dec_t, IDX_DIM), srs, sis)
    return prompt_out, sample_out


def kernel(x_prompt, x_sample, cache_k, cache_v, cache_kidx, state_ssm_re, state_ssm_im, page_table, norm_mix_pre, w_in, lambda_re, lambda_im, log_dt, b_re, b_im, c_re, c_im, d_skip, w_glu, b_glu, w_out, norm_mix_post, norm_ffn_pre, w_up, w_down, norm_ffn_post):
    depth = w_in.shape[0]
    hp, hs = x_prompt, x_sample
    p_acc = [[] for _ in range(5)]
    s_acc = [[] for _ in range(5)]
    for l in range(depth):
        ssm_params = (lambda_re[l], lambda_im[l], log_dt[l], b_re[l], b_im[l], c_re[l], c_im[l])
        p_out, s_out = _layer(hp, hs, cache_k[l], cache_v[l], cache_kidx[l], state_ssm_re[l], state_ssm_im[l],
                              page_table, norm_mix_pre[l], w_in[l], ssm_params, d_skip[l], w_glu[l], b_glu[l],
                              w_out[l], norm_mix_post[l], norm_ffn_pre[l], w_up[l], w_down[l], norm_ffn_post[l])
        hp, hs = p_out[0], s_out[0]
        for acc, vals in ((p_acc, p_out[1:]), (s_acc, s_out[1:])):
            for a, v in zip(acc, vals):
                a.append(v)
    return (hp, hs) + tuple(jnp.stack(a) for a in p_acc) + tuple(jnp.stack(a) for a in s_acc)
```

```python
import functools
import math

import jax
import jax.numpy as jnp
from jax import lax
from jax.experimental import pallas as pl
from jax.experimental.pallas import tpu as pltpu

F32 = jnp.float32
BF16 = jnp.bfloat16
I32 = jnp.int32

D_MODEL = 1024
PAGE_SIZE = 128
HEAD_DIM = 64
ATTN_WIDTH = D_MODEL // 2
N_HEADS = ATTN_WIDTH // HEAD_DIM
N_KV_HEADS = 2
Q_PER_KV = N_HEADS // N_KV_HEADS
KV_WIDTH = N_KV_HEADS * HEAD_DIM
N_IDX_HEADS = 4
IDX_DIM = 64
TOPK_MAX = 256
ROPE_THETA = 10000.0
SSM_WIDTH = D_MODEL - ATTN_WIDTH
SSM_GROUP = 16
N_SSM_GROUPS = SSM_WIDTH // SSM_GROUP
SSM_STATE = 64
D_FF = 4 * D_MODEL
RMS_EPS = 1e-6
NEG = -1e30
OFF_Q = 0
OFF_K = OFF_Q + ATTN_WIDTH
OFF_V = OFF_K + KV_WIDTH
OFF_QI = OFF_V + KV_WIDTH
OFF_KI = OFF_QI + N_IDX_HEADS * IDX_DIM
OFF_WI = OFF_KI + IDX_DIM
OFF_U = OFF_WI + N_IDX_HEADS
IN_WIDTH = OFF_U + SSM_WIDTH

LANES = 128
SUBLANES = 8
INT_MIN = -(2 ** 31)
VMEM_LIMIT = 56 * 1024 * 1024

QPAD = N_HEADS * LANES
MAIN_W = QPAD + 2 * KV_WIDTH + SSM_WIDTH
IDX_W = 2 * N_IDX_HEADS * IDX_DIM + 2 * IDX_DIM + LANES
SSM_LANES = N_SSM_GROUPS * SSM_STATE
N_SLAB = 2 * SSM_LANES // LANES
N_UCHUNK = SSM_WIDTH // LANES
GROUPS_PER_CHUNK = LANES // SSM_GROUP
SLABS_PER_CHUNK = N_SLAB // N_UCHUNK

NT_DIMS = (((1,), (1,)), ((), ()))
LOG2_E = math.log2(math.e)
I16 = jnp.int16
I16_MIN = -(2 ** 15)
I16_MAX = 2 ** 15 - 1


def _cparams(n_grid):
    return pltpu.CompilerParams(dimension_semantics=("arbitrary",) * n_grid,
                                vmem_limit_bytes=VMEM_LIMIT)


def _const_spec(shape):
    zeros = (0,) * len(shape)
    return pl.BlockSpec(shape, lambda *_: zeros, pipeline_mode=pl.Buffered(1))


def _rms(x, g):
    return x * lax.rsqrt(jnp.mean(x * x, axis=-1, keepdims=True) + RMS_EPS) * g


def _split_bf16(x):
    hi = x.astype(BF16)
    lo = (x - hi.astype(F32)).astype(BF16)
    return hi, lo


def _sortable(x):
    b = lax.bitcast_convert_type(x, I32)
    return b ^ ((b >> 31) & jnp.int32(0x7FFFFFFF))


def _rope(x, cos, sin):
    w = x.shape[-1]
    lane = lax.broadcasted_iota(I32, x.shape, 1)
    first = (lane & (HEAD_DIM - 1)) < HEAD_DIM // 2
    sw = jnp.where(first, pltpu.roll(x, w - HEAD_DIM // 2, 1), pltpu.roll(x, HEAD_DIM // 2, 1))
    return x * cos + sw * sin


def _inproj_kernel(x_ref, g_ref, wm_ref, wih_ref, wil_ref, cos_ref, sin_ref,
                   q_ref, kf_ref, kb_ref, vf_ref, vt_ref, u_ref, kif_ref, ki4_ref, qi4_ref, misc_ref):
    xn = _rms(x_ref[...], g_ref[...])
    xh, xl = _split_bf16(xn)
    cos = cos_ref[...]
    sin = sin_ref[...]
    main = jnp.dot(xh, wm_ref[...], preferred_element_type=F32)
    wih = wih_ref[...]
    idx = (jnp.dot(xh, wih, preferred_element_type=F32) + jnp.dot(xl, wih, preferred_element_type=F32)
           + jnp.dot(xh, wil_ref[...], preferred_element_type=F32))

    scale = HEAD_DIM ** -0.5 * LOG2_E
    for h in range(N_HEADS):
        blk = main[:, h * LANES:(h + 1) * LANES]
        q_ref[:, h * LANES:(h + 1) * LANES] = (_rope(blk, cos, sin) * scale).astype(BF16)
    k = _rope(main[:, QPAD:QPAD + KV_WIDTH], cos, sin)
    kf_ref[...] = k.T
    kb_ref[...] = k.astype(BF16)
    vt = main[:, QPAD + KV_WIDTH:QPAD + 2 * KV_WIDTH].T
    vf_ref[...] = vt
    vt_ref[...] = vt.astype(BF16)
    u_ref[...] = main[:, QPAD + 2 * KV_WIDTH:]

    for h in range(N_IDX_HEADS):
        d = _rope(idx[:, h * LANES:(h + 1) * LANES], cos, sin)
        hi, lo = _split_bf16(d)
        qi4_ref[:, 2 * h * LANES:(2 * h + 1) * LANES] = hi
        qi4_ref[:, (2 * h + 1) * LANES:(2 * h + 2) * LANES] = lo
    off = N_IDX_HEADS * LANES
    d = _rope(idx[:, off:off + LANES], cos, sin)
    kif_ref[...] = d.T[:IDX_DIM, :]
    hi = d.astype(BF16).astype(F32)
    lane = lax.broadcasted_iota(I32, d.shape, 1)
    hl = jnp.where(lane < IDX_DIM, hi, d - hi).astype(BF16)
    ki4_ref[:, :LANES] = hl
    ki4_ref[:, LANES:] = hl
    misc_ref[...] = idx[:, off + LANES:]


def _inproj(x, pos, g_pre, wm, wih, wil, tm):
    B, T, _ = x.shape
    half = HEAD_DIM // 2
    inv = ROPE_THETA ** (-jnp.arange(half, dtype=F32) / half)
    ang = pos.astype(F32)[:, None] * inv[None, :]
    cos = jnp.tile(jnp.cos(ang), (1, LANES // half))
    s = jnp.sin(ang)
    sin = jnp.tile(jnp.concatenate([-s, s], axis=1), (1, LANES // HEAD_DIM))

    def row(w):
        return pl.BlockSpec((None, tm, w), lambda b, t: (b, t, 0))

    def tab():
        return pl.BlockSpec((tm, LANES), lambda b, t: (t, 0))

    def col(w):
        return pl.BlockSpec((None, w, tm), lambda b, t: (b, 0, t))

    out_shapes = (
        jax.ShapeDtypeStruct((B, T, QPAD), BF16),
        jax.ShapeDtypeStruct((B, KV_WIDTH, T), F32),
        jax.ShapeDtypeStruct((B, T, KV_WIDTH), BF16),
        jax.ShapeDtypeStruct((B, KV_WIDTH, T), F32),
        jax.ShapeDtypeStruct((B, KV_WIDTH, T), BF16),
        jax.ShapeDtypeStruct((T, B * SSM_WIDTH), F32),
        jax.ShapeDtypeStruct((B, IDX_DIM, T), F32),
        jax.ShapeDtypeStruct((B, T, 4 * IDX_DIM), BF16),
        jax.ShapeDtypeStruct((B, T, 4 * IDX_DIM * N_IDX_HEADS), BF16),
        jax.ShapeDtypeStruct((B, T, LANES), F32),
    )
    out_specs = (
        row(QPAD), col(KV_WIDTH), row(KV_WIDTH), col(KV_WIDTH), col(KV_WIDTH),
        pl.BlockSpec((tm, SSM_WIDTH), lambda b, t: (t, b)),
        col(IDX_DIM), row(4 * IDX_DIM), row(4 * IDX_DIM * N_IDX_HEADS), row(LANES),
    )
    return pl.pallas_call(
        _inproj_kernel,
        grid=(B, T // tm),
        in_specs=[row(D_MODEL), _const_spec((1, D_MODEL)), _const_spec((D_MODEL, MAIN_W)),
                  _const_spec((D_MODEL, IDX_W)), _const_spec((D_MODEL, IDX_W)), tab(), tab()],
        out_specs=out_specs,
        out_shape=out_shapes,
        compiler_params=_cparams(2),
        name="inproj",
    )(x, g_pre.reshape(1, D_MODEL), wm, wih, wil, cos, sin)


def _inproj_weights(w_in):
    cols = []
    zero = jnp.zeros((D_MODEL, HEAD_DIM), F32)
    for h in range(N_HEADS):
        wq = w_in[:, OFF_Q + h * HEAD_DIM:OFF_Q + (h + 1) * HEAD_DIM]
        cols += [wq, zero] if h // Q_PER_KV == 0 else [zero, wq]
    cols += [w_in[:, OFF_K:OFF_QI], w_in[:, OFF_U:]]
    wm = jnp.concatenate(cols, axis=1).astype(BF16)
    icols = []
    for h in range(N_IDX_HEADS):
        wq = w_in[:, OFF_QI + h * IDX_DIM:OFF_QI + (h + 1) * IDX_DIM]
        icols += [wq, wq]
    wk = w_in[:, OFF_KI:OFF_WI]
    icols += [wk, wk, w_in[:, OFF_WI:OFF_U], jnp.zeros((D_MODEL, LANES - N_IDX_HEADS), F32)]
    wi = jnp.concatenate(icols, axis=1)
    wih = wi.astype(BF16)
    wil = (wi - wih.astype(F32)).astype(BF16)
    return wm, wih, wil


def _attn_kernel(qi4_ref, wit_ref, ki4_ref, q_ref, kb_ref, vt_ref, o_ref,
                 keys_ref, khi_ref, klo_ref, qs_ref, lg_ref, bias_ref, p_ref, alpha_ref, tie_ref,
                 m_ref, l_ref, acc_ref, *, topk, tq):
    i = pl.program_id(1)
    nk = i + 1
    w = wit_ref[...]

    def chunk(c):
        return pl.ds(pl.multiple_of(c * tq, tq), tq)

    def score_body(c, _):
        kc = ki4_ref[chunk(c), :]
        sc = None
        for h in range(N_IDX_HEADS):
            s = lax.dot_general(kc, qi4_ref[:, 4 * IDX_DIM * h:4 * IDX_DIM * (h + 1)], NT_DIMS,
                                preferred_element_type=F32)
            t = jnp.maximum(s, 0.0) * w[h:h + 1, :]
            sc = t if sc is None else sc + t
        key = _sortable(sc)
        kpos = c * tq + lax.broadcasted_iota(I32, key.shape, 0)
        qpos = i * tq + lax.broadcasted_iota(I32, key.shape, 1)
        key = jnp.where(kpos <= qpos, key, INT_MIN)
        keys_ref[chunk(c), :] = key
        khi_ref[chunk(c), :] = (key >> 16).astype(I16)
        return 0

    lax.fori_loop(0, nk, score_body, 0)

    pack_rows = 2 * SUBLANES
    n_acc = 4

    def count16(ref, cand):
        cand = cand.astype(I16)

        def body(c, accs):
            m = jnp.where(ref[chunk(c), :] >= cand, jnp.ones((), I16), jnp.zeros((), I16))
            m = m.reshape(tq // pack_rows, pack_rows, tq)
            accs = list(accs)
            for r in range(tq // pack_rows):
                accs[r % n_acc] = accs[r % n_acc] + m[r]
            return tuple(accs)

        accs = lax.fori_loop(0, nk, body, tuple(jnp.zeros((pack_rows, tq), I16) for _ in range(n_acc)))
        tot = (accs[0] + accs[1]) + (accs[2] + accs[3])
        return jnp.sum(tot.astype(I32), axis=0, keepdims=True)

    def search16(ref):
        def bit_body(b, t):
            cand = t + (jnp.int32(1) << (15 - b))
            return jnp.where(count16(ref, cand) >= topk, cand, t)
        return lax.fori_loop(0, 16, bit_body, jnp.full((1, tq), I16_MIN, I32))

    tau_hi = search16(khi_ref)

    def lo_body(c, _):
        k = keys_ref[chunk(c), :]
        hi = k >> 16
        lo = (k & 0xFFFF) + I16_MIN
        lo = jnp.where(hi == tau_hi, lo, jnp.where(hi > tau_hi, I16_MAX, I16_MIN))
        klo_ref[chunk(c), :] = lo.astype(I16)
        return 0

    lax.fori_loop(0, nk, lo_body, 0)
    tau_lo = search16(klo_ref)
    tau = (tau_hi << 16) + (tau_lo - I16_MIN)
    tau = jnp.maximum(tau, INT_MIN + 1)

    def gt_body(c, acc):
        m = jnp.where(keys_ref[chunk(c), :] > tau, 1, 0).astype(I32)
        return acc + jnp.sum(m.reshape(tq // SUBLANES, SUBLANES, tq), axis=0)

    n_gt = jnp.sum(lax.fori_loop(0, nk, gt_body, jnp.zeros((SUBLANES, tq), I32)), axis=0, keepdims=True)
    quota = (topk - n_gt).astype(F32)

    for h in range(N_HEADS):
        qs_ref[h * tq:(h + 1) * tq, :] = q_ref[:, h * LANES:(h + 1) * LANES]
    m_ref[...] = jnp.full(m_ref.shape, NEG, F32)
    l_ref[...] = jnp.zeros(l_ref.shape, F32)
    acc_ref[...] = jnp.zeros(acc_ref.shape, F32)
    r_i = lax.broadcasted_iota(I32, (tq, tq), 0)
    c_i = lax.broadcasted_iota(I32, (tq, tq), 1)
    ltri = jnp.where(c_i <= r_i, 1.0, 0.0).astype(BF16)

    def prepare(c, slot):
        cc = jnp.minimum(c, nk - 1)
        kc = keys_ref[chunk(cc), :]
        eq = kc == tau
        e = jnp.where(eq, 1.0, 0.0)
        pre = jnp.dot(ltri, e.astype(BF16), preferred_element_type=F32)
        rank = tie_ref[...] + pre - e
        sel = ((kc > tau) | (eq & (rank < quota))) & (c < nk)
        tie_ref[...] = tie_ref[...] + pre[tq - 1:tq, :]
        bias_ref[slot] = jnp.where(sel, 0.0, NEG)
        lg_ref[slot] = lax.dot_general(kb_ref[chunk(cc), :], qs_ref[...], NT_DIMS, preferred_element_type=F32)

    def softmax(slot):
        for j in range(N_HEADS * tq // LANES):
            cols = slice(j * LANES, (j + 1) * LANES)
            jb = j % (tq // LANES)
            x = lg_ref[slot, :, cols] + bias_ref[slot, :, jb * LANES:(jb + 1) * LANES]
            m_prev = m_ref[:, cols]
            m_new = jnp.maximum(m_prev, jnp.max(x, axis=0, keepdims=True))
            alpha = jnp.exp2(m_prev - m_new)
            p = jnp.exp2(x - m_new)
            l_ref[:, cols] = alpha * l_ref[:, cols] + jnp.sum(p, axis=0, keepdims=True)
            m_ref[:, cols] = m_new
            p_ref[slot, :, cols] = p.astype(BF16)
            alpha_ref[slot, :, cols] = alpha

    def value(c, slot):
        cc = jnp.clip(c, 0, nk - 1)
        acc_ref[...] = alpha_ref[slot] * acc_ref[...] + jnp.dot(vt_ref[:, chunk(cc)], p_ref[slot],
                                                                preferred_element_type=F32)

    tie_ref[...] = jnp.zeros(tie_ref.shape, F32)
    p_ref[1] = jnp.zeros(p_ref.shape[1:], BF16)
    alpha_ref[1] = jnp.ones(alpha_ref.shape[1:], F32)
    prepare(0, 0)

    def pair_body(cc, _):
        for slot in range(2):
            c = 2 * cc + slot
            prepare(c + 1, 1 - slot)
            softmax(slot)
            value(c - 1, 1 - slot)
        return 0

    n_pairs = (nk + 1) // 2
    lax.fori_loop(0, n_pairs, pair_body, 0)
    value(2 * n_pairs - 1, 1)
    out = acc_ref[...] / l_ref[...]
    for h in range(N_HEADS):
        o_ref[:, h * LANES:(h + 1) * LANES] = out[:, h * tq:(h + 1) * tq].T.astype(BF16)


def _prompt_attention(qi4, wit, ki4, q, kb, vt, tq):
    B, T, _ = q.shape
    topk = min(TOPK_MAX, T // 4)

    def tile(w):
        return pl.BlockSpec((None, tq, w), lambda b, i: (b, i, 0))

    def full(w):
        return pl.BlockSpec((None, T, w), lambda b, i: (b, 0, 0))

    return pl.pallas_call(
        functools.partial(_attn_kernel, topk=topk, tq=tq),
        grid=(B, T // tq),
        in_specs=[tile(4 * IDX_DIM * N_IDX_HEADS),
                  pl.BlockSpec((None, N_IDX_HEADS, tq), lambda b, i: (b, 0, i)),
                  full(4 * IDX_DIM), tile(QPAD), full(KV_WIDTH),
                  pl.BlockSpec((None, KV_WIDTH, T), lambda b, i: (b, 0, 0))],
        out_specs=tile(QPAD),
        out_shape=jax.ShapeDtypeStruct((B, T, QPAD), BF16),
        scratch_shapes=[pltpu.VMEM((T, tq), I32), pltpu.VMEM((T, tq), I16), pltpu.VMEM((T, tq), I16),
                        pltpu.VMEM((N_HEADS * tq, LANES), BF16), pltpu.VMEM((2, tq, N_HEADS * tq), F32),
                        pltpu.VMEM((2, tq, tq), F32), pltpu.VMEM((2, tq, N_HEADS * tq), BF16),
                        pltpu.VMEM((2, 1, N_HEADS * tq), F32), pltpu.VMEM((1, tq), F32),
                        pltpu.VMEM((1, N_HEADS * tq), F32), pltpu.VMEM((1, N_HEADS * tq), F32),
                        pltpu.VMEM((LANES, N_HEADS * tq), F32)],
        compiler_params=_cparams(2),
        name="prompt_attention",
    )(qi4, wit, ki4, q, kb, vt)


def _s5_prep_kernel(lr_ref, li_ref, ldt_ref, brt_ref, bit_ref, ar_ref, ai_ref, bbr_ref, bbi_ref):
    lr = lr_ref[...]
    li = li_ref[...]
    dt = jnp.exp(ldt_ref[...])
    mag = jnp.exp(lr * dt)
    ar = mag * jnp.cos(li * dt)
    ai = mag * jnp.sin(li * dt)
    den = lr * lr + li * li
    cr = ((ar - 1.0) * lr + ai * li) / den
    ci = (ai * lr - (ar - 1.0) * li) / den
    ar_ref[...] = ar
    ai_ref[...] = ai
    brt = brt_ref[...]
    bit = bit_ref[...]
    bbr_ref[...] = cr[:, None, :] * brt - ci[:, None, :] * bit
    bbi_ref[...] = cr[:, None, :] * bit + ci[:, None, :] * brt


def _s5_weights(lambda_re, lambda_im, log_dt, b_re, b_im, c_re, c_im):
    G, P, C = N_SSM_GROUPS, SSM_STATE, SSM_GROUP
    ar, ai, bbr, bbi = pl.pallas_call(
        _s5_prep_kernel,
        out_shape=(jax.ShapeDtypeStruct((G, P), F32), jax.ShapeDtypeStruct((G, P), F32),
                   jax.ShapeDtypeStruct((G, C, P), F32), jax.ShapeDtypeStruct((G, C, P), F32)),
        name="s5_prep",
    )(lambda_re, lambda_im, log_dt.reshape(G, 1), jnp.swapaxes(b_re, 1, 2), jnp.swapaxes(b_im, 1, 2))
    eye = jnp.eye(GROUPS_PER_CHUNK, dtype=F32)

    def in_blocks(bb):
        x = bb.reshape(N_UCHUNK, GROUPS_PER_CHUNK, C, P)
        x = x[:, :, :, None, :] * eye[None, :, None, :, None]
        return x.reshape(N_UCHUNK, LANES, GROUPS_PER_CHUNK * P)

    def out_blocks(cc):
        x = jnp.swapaxes(cc, 1, 2).reshape(N_UCHUNK, GROUPS_PER_CHUNK, P, C)
        x = x[:, :, :, None, :] * eye[None, :, None, :, None]
        return x.reshape(N_UCHUNK, GROUPS_PER_CHUNK * P, LANES)

    bblk = jnp.concatenate([in_blocks(bbr), in_blocks(bbi)], axis=2).astype(BF16)
    cblk = jnp.concatenate([out_blocks(c_re), out_blocks(-c_im)], axis=1).astype(BF16)
    half = SLABS_PER_CHUNK // 2

    def slabs(a):
        x = a.reshape(N_UCHUNK, 1, half, 1, LANES)
        return jnp.broadcast_to(x, (N_UCHUNK, 2, half, 1, LANES)).reshape(N_SLAB, 1, LANES)

    return slabs(ar), slabs(ai), bblk, cblk


def _state_to_slabs(h_re, h_im):
    B = h_re.shape[0]
    half = SLABS_PER_CHUNK // 2
    re = h_re.reshape(B, N_UCHUNK, 1, half, LANES)
    im = h_im.reshape(B, N_UCHUNK, 1, half, LANES)
    x = jnp.concatenate([re, im], axis=2).reshape(B, N_SLAB, LANES)
    return jnp.swapaxes(x, 0, 1)


def _slabs_to_state(h):
    B = h.shape[1]
    half = SLABS_PER_CHUNK // 2
    x = jnp.swapaxes(h, 0, 1).reshape(B, N_UCHUNK, 2, half * LANES)
    re = x[:, :, 0].reshape(B, N_SSM_GROUPS, SSM_STATE)
    im = x[:, :, 1].reshape(B, N_SSM_GROUPS, SSM_STATE)
    return re, im


def _s5_kernel(u_ref, h0_ref, ar_ref, ai_ref, bblk_ref, cblk_ref, d_ref, y_ref, hout_ref, hb_ref,
               *, nb, tj, base, lane_batches):
    rows = tj * nb
    half = SLABS_PER_CHUNK // 2
    blk_rows = rows // lane_batches

    @pl.when(pl.program_id(0) == 0)
    def _():
        hb_ref[:, base - nb:base, :] = h0_ref[...]

    def batch_rows(lb):
        if lane_batches == 1:
            return pl.ds(base, blk_rows)
        return pl.ds(base + lb, blk_rows, stride=lane_batches)

    for lb in range(lane_batches):
        for q in range(N_UCHUNK):
            cols = slice(lb * SSM_WIDTH + q * LANES, lb * SSM_WIDTH + (q + 1) * LANES)
            bu = jnp.dot(u_ref[:, cols].astype(BF16), bblk_ref[q], preferred_element_type=F32)
            for s in range(SLABS_PER_CHUNK):
                hb_ref[q * SLABS_PER_CHUNK + s, batch_rows(lb), :] = bu[:, s * LANES:(s + 1) * LANES]

    def step(j, _):
        prev = pl.ds(base + (j - 1) * nb, nb)
        cur = pl.ds(base + j * nb, nb)
        for q in range(N_UCHUNK):
            for r in range(half):
                sr = q * SLABS_PER_CHUNK + r
                si = sr + half
                a_re = ar_ref[sr]
                a_im = ai_ref[sr]
                p_re = hb_ref[sr, prev, :]
                p_im = hb_ref[si, prev, :]
                hb_ref[sr, cur, :] = a_re * p_re - a_im * p_im + hb_ref[sr, cur, :]
                hb_ref[si, cur, :] = a_re * p_im + a_im * p_re + hb_ref[si, cur, :]
        return 0

    lax.fori_loop(0, tj, step, 0)

    for lb in range(lane_batches):
        for q in range(N_UCHUNK):
            hq = jnp.concatenate([hb_ref[q * SLABS_PER_CHUNK + s, batch_rows(lb), :]
                                  for s in range(SLABS_PER_CHUNK)], axis=1).astype(BF16)
            yq = jnp.dot(hq, cblk_ref[q], preferred_element_type=F32)
            cols = slice(lb * SSM_WIDTH + q * LANES, lb * SSM_WIDTH + (q + 1) * LANES)
            y_ref[:, cols] = yq + d_ref[:, q * LANES:(q + 1) * LANES] * u_ref[:, cols]
    last = hb_ref[:, base + rows - nb:base + rows, :]
    hout_ref[...] = last
    hb_ref[:, base - nb:base, :] = last


def _s5_scan(u, nb, lane_batches, h0_slabs, weights, d_skip, tj):
    T = u.shape[0] * lane_batches // nb
    ar, ai, bblk, cblk = weights
    rows = tj * nb
    base = max(SUBLANES, nb)
    blk = (rows // lane_batches, lane_batches * SSM_WIDTH)
    y, hout = pl.pallas_call(
        functools.partial(_s5_kernel, nb=nb, tj=tj, base=base, lane_batches=lane_batches),
        grid=(T // tj,),
        in_specs=[pl.BlockSpec(blk, lambda t: (t, 0)),
                  _const_spec((N_SLAB, nb, LANES)), _const_spec((N_SLAB, 1, LANES)),
                  _const_spec((N_SLAB, 1, LANES)),
                  _const_spec((N_UCHUNK, LANES, SLABS_PER_CHUNK * LANES)),
                  _const_spec((N_UCHUNK, SLABS_PER_CHUNK * LANES, LANES)),
                  _const_spec((1, SSM_WIDTH))],
        out_specs=(pl.BlockSpec(blk, lambda t: (t, 0)),
                   pl.BlockSpec((N_SLAB, nb, LANES), lambda t: (0, 0, 0))),
        out_shape=(jax.ShapeDtypeStruct(u.shape, F32),
                   jax.ShapeDtypeStruct((N_SLAB, nb, LANES), F32)),
        scratch_shapes=[pltpu.VMEM((N_SLAB, base + rows, LANES), F32)],
        compiler_params=_cparams(1),
        name="s5_scan",
    )(u, h0_slabs, ar, ai, bblk, cblk, d_skip.reshape(1, SSM_WIDTH))
    return y, hout


def _post_kernel(x_ref, ao_ref, y_ref, wglu_ref, bglu_ref, woa_ref, wos_ref, gpost_ref, gpre_ref,
                 wup_ref, wdn_ref, gfpost_ref, o_ref):
    z = jax.nn.gelu(y_ref[...])
    gate = jax.nn.sigmoid(jnp.dot(z.astype(BF16), wglu_ref[...], preferred_element_type=F32) + bglu_ref[...])
    so = z * gate
    mix = (jnp.dot(ao_ref[...], woa_ref[...], preferred_element_type=F32)
           + jnp.dot(so.astype(BF16), wos_ref[...], preferred_element_type=F32))
    x = x_ref[...] + _rms(mix, gpost_ref[...])
    hdn = jnp.dot(_rms(x, gpre_ref[...]).astype(BF16), wup_ref[...], preferred_element_type=F32)
    hdn = jnp.square(jnp.maximum(hdn, 0.0))
    dn = jnp.dot(hdn.astype(BF16), wdn_ref[...], preferred_element_type=F32)
    o_ref[...] = x + _rms(dn, gfpost_ref[...])


def _post_weights(w_glu, w_out, w_up, w_down):
    zero = jnp.zeros((HEAD_DIM, D_MODEL), F32)
    rows = []
    for h in range(N_HEADS):
        wh = w_out[h * HEAD_DIM:(h + 1) * HEAD_DIM]
        rows += [wh, zero] if h // Q_PER_KV == 0 else [zero, wh]
    woa = jnp.concatenate(rows, axis=0).astype(BF16)
    return (w_glu.astype(BF16), woa, w_out[ATTN_WIDTH:].astype(BF16), w_up.astype(BF16), w_down.astype(BF16))


def _post(x, ao, y_tb, weights, b_glu, g_post, g_ffn_pre, g_ffn_post, tm):
    B, T, _ = x.shape
    wglu, woa, wos, wup, wdn = weights

    def row(w):
        return pl.BlockSpec((None, tm, w), lambda b, t: (b, t, 0))

    def vec(a):
        return a.reshape(1, -1)

    return pl.pallas_call(
        _post_kernel,
        grid=(B, T // tm),
        in_specs=[row(D_MODEL), row(QPAD), pl.BlockSpec((tm, SSM_WIDTH), lambda b, t: (t, b)),
                  _const_spec((SSM_WIDTH, SSM_WIDTH)), _const_spec((1, SSM_WIDTH)),
                  _const_spec((QPAD, D_MODEL)), _const_spec((SSM_WIDTH, D_MODEL)),
                  _const_spec((1, D_MODEL)), _const_spec((1, D_MODEL)),
                  _const_spec((D_MODEL, D_FF)), _const_spec((D_FF, D_MODEL)), _const_spec((1, D_MODEL))],
        out_specs=row(D_MODEL),
        out_shape=jax.ShapeDtypeStruct((B, T, D_MODEL), F32),
        compiler_params=_cparams(2),
        name="post",
    )(x, ao, y_tb, wglu, vec(b_glu), woa, wos, vec(g_post), vec(g_ffn_pre), wup, wdn, vec(g_ffn_post))


def _page_map(b, j, pt, *, r, n):
    return (pt[b, j * n + r], 0, 0)


def _sel_kernel(pt_ref, qh_ref, ql_ref, w_ref, kn_ref, *rest, topk, n_pages_step, past_len, dec_t):
    page_refs = rest[:n_pages_step]
    sel_ref = rest[n_pages_step]
    keys_ref = rest[n_pages_step + 1]
    j = pl.program_id(1)
    step_keys = n_pages_step * PAGE_SIZE
    qh = qh_ref[...]
    ql = ql_ref[...]
    w = w_ref[...]

    def scores(kc):
        kh, kl = _split_bf16(kc)
        s = (jnp.dot(qh, kh, preferred_element_type=F32) + jnp.dot(ql, kh, preferred_element_type=F32)
             + jnp.dot(qh, kl, preferred_element_type=F32))
        sc = None
        for h in range(N_IDX_HEADS):
            t = jnp.maximum(s[h * dec_t:(h + 1) * dec_t], 0.0) * w[:, h:h + 1]
            sc = t if sc is None else sc + t
        return _sortable(sc)

    kc = jnp.concatenate([r[...] for r in page_refs], axis=1)
    keys_ref[:, pl.ds(pl.multiple_of(j * step_keys, step_keys), step_keys)] = scores(kc)

    @pl.when(j == pl.num_programs(1) - 1)
    def _():
        kn = scores(kn_ref[...])
        qi_ = lax.broadcasted_iota(I32, kn.shape, 0)
        ki_ = lax.broadcasted_iota(I32, kn.shape, 1)
        keys_ref[:, past_len:] = jnp.where(ki_ <= qi_, kn, INT_MIN)
        keys = keys_ref[...]

        def count(m):
            return jnp.sum(jnp.where(m, 1, 0).astype(I32), axis=1, keepdims=True)

        tau = jnp.where(count(keys >= 0) >= topk, 0, INT_MIN).astype(I32)

        def bit_body(b, tau):
            cand = tau + (jnp.int32(1) << (30 - b))
            return jnp.where(count(keys >= cand) >= topk, cand, tau)

        tau = lax.fori_loop(0, 31, bit_body, tau)
        tau = jnp.maximum(tau, INT_MIN + 1)
        gt = keys > tau
        eq = keys == tau
        quota = topk - count(gt)
        pos = lax.broadcasted_iota(I32, keys.shape, 1)
        n_bits = max(1, (keys.shape[1] - 1).bit_length())

        def idx_body(b, lim):
            cand = lim + (jnp.int32(1) << (n_bits - 1 - b))
            return jnp.where(count(eq & (pos < cand)) < quota, cand, lim)

        lim = lax.fori_loop(0, n_bits, idx_body, jnp.zeros_like(tau))
        sel_ref[...] = jnp.where(gt | (eq & (pos <= lim)), 1.0, 0.0)


def _sample_select(page_table, qh, ql, w, ki_new_pad, cache_kidx, n_pages_step):
    DB, n_pages = page_table.shape
    dec_t = w.shape[1]
    past_len = n_pages * PAGE_SIZE
    lp = past_len + LANES
    topk = min(TOPK_MAX, (past_len + dec_t) // 4)
    rows = N_IDX_HEADS * dec_t

    def fixed(shape):
        return pl.BlockSpec((None,) + shape, lambda b, j, pt: (b, 0, 0))

    pages = [pl.BlockSpec((None, IDX_DIM, PAGE_SIZE), functools.partial(_page_map, r=r, n=n_pages_step))
             for r in range(n_pages_step)]
    grid_spec = pltpu.PrefetchScalarGridSpec(
        num_scalar_prefetch=1,
        grid=(DB, n_pages // n_pages_step),
        in_specs=[fixed((rows, IDX_DIM)), fixed((rows, IDX_DIM)), fixed((dec_t, N_IDX_HEADS)),
                  fixed((IDX_DIM, LANES))] + pages,
        out_specs=fixed((dec_t, lp)),
        scratch_shapes=[pltpu.VMEM((dec_t, lp), I32)],
    )
    return pl.pallas_call(
        functools.partial(_sel_kernel, topk=topk, n_pages_step=n_pages_step, past_len=past_len, dec_t=dec_t),
        grid_spec=grid_spec,
        out_shape=jax.ShapeDtypeStruct((DB, dec_t, lp), F32),
        compiler_params=_cparams(2),
        name="sample_select",
    )(page_table, qh, ql, w, ki_new_pad, *([cache_kidx] * n_pages_step))


def _sattn_kernel(pt_ref, sel_ref, q_ref, kn_ref, vn_ref, *rest, n_pages_step, past_len, dec_t):
    k_refs = rest[:n_pages_step]
    v_refs = rest[n_pages_step:2 * n_pages_step]
    o_ref, m_ref, l_ref, acc_ref = rest[2 * n_pages_step:]
    j = pl.program_id(1)
    step_keys = n_pages_step * PAGE_SIZE
    q = q_ref[...]

    @pl.when(j == 0)
    def _():
        m_ref[...] = jnp.full(m_ref.shape, NEG, F32)
        l_ref[...] = jnp.zeros(l_ref.shape, F32)
        acc_ref[...] = jnp.zeros(acc_ref.shape, F32)

    def update(kc, vc, sel):
        n = kc.shape[1]
        lg = jnp.dot(q, kc, preferred_element_type=F32)
        lg = jnp.where(sel[None] > 0.5, lg.reshape(N_HEADS, dec_t, n), NEG).reshape(N_HEADS * dec_t, n)
        m_prev = m_ref[...]
        m_new = jnp.maximum(m_prev, jnp.max(lg, axis=1, keepdims=True))
        alpha = jnp.exp2(m_prev - m_new)
        p = jnp.exp2(lg - m_new)
        l_ref[...] = alpha * l_ref[...] + jnp.sum(p, axis=1, keepdims=True)
        acc_ref[...] = alpha * acc_ref[...] + lax.dot_general(p.astype(BF16), vc, NT_DIMS,
                                                              preferred_element_type=F32)
        m_ref[...] = m_new

    kc = jnp.concatenate([r[...] for r in k_refs], axis=1).astype(BF16)
    vc = jnp.concatenate([r[...] for r in v_refs], axis=1).astype(BF16)
    update(kc, vc, sel_ref[:, pl.ds(pl.multiple_of(j * step_keys, step_keys), step_keys)])

    @pl.when(j == pl.num_programs(1) - 1)
    def _():
        update(kn_ref[...], vn_ref[...], sel_ref[:, past_len:])
        o_ref[...] = acc_ref[...] / l_ref[...]


def _sample_attention(page_table, sel, q, k_new_pad, v_new_pad, cache_k, cache_v, n_pages_step):
    DB, n_pages = page_table.shape
    dec_t = sel.shape[1]
    past_len = n_pages * PAGE_SIZE
    lp = past_len + LANES
    rows = N_HEADS * dec_t

    def fixed(shape):
        return pl.BlockSpec((None,) + shape, lambda b, j, pt: (b, 0, 0))

    def pages():
        return [pl.BlockSpec((None, KV_WIDTH, PAGE_SIZE), functools.partial(_page_map, r=r, n=n_pages_step))
                for r in range(n_pages_step)]

    grid_spec = pltpu.PrefetchScalarGridSpec(
        num_scalar_prefetch=1,
        grid=(DB, n_pages // n_pages_step),
        in_specs=[fixed((dec_t, lp)), fixed((rows, KV_WIDTH)), fixed((KV_WIDTH, LANES)),
                  fixed((KV_WIDTH, LANES))] + pages() + pages(),
        out_specs=fixed((rows, KV_WIDTH)),
        scratch_shapes=[pltpu.VMEM((rows, 1), F32), pltpu.VMEM((rows, 1), F32),
                        pltpu.VMEM((rows, KV_WIDTH), F32)],
    )
    return pl.pallas_call(
        functools.partial(_sattn_kernel, n_pages_step=n_pages_step, past_len=past_len, dec_t=dec_t),
        grid_spec=grid_spec,
        out_shape=jax.ShapeDtypeStruct((DB, rows, KV_WIDTH), F32),
        compiler_params=_cparams(2),
        name="sample_attention",
    )(page_table, sel, q, k_new_pad, v_new_pad, *([cache_k] * n_pages_step), *([cache_v] * n_pages_step))


def _largest_tile(n, cap):
    t = min(n, cap)
    while n % t:
        t //= 2
    return t


def _layer(hp, hs, cache_k, cache_v, cache_kidx, st_re, st_im, page_table, g_pre, w_in, ssm_params, d_skip,
           w_glu, b_glu, w_out, g_post, g_ffn_pre, w_up, w_down, g_ffn_post):
    B, T, _ = hp.shape
    DB, dec_t, _ = hs.shape
    n_pages = page_table.shape[1]
    past_len = n_pages * PAGE_SIZE
    idx_scale = (IDX_DIM ** -0.5) * (N_IDX_HEADS ** -0.5)

    wm, wih, wil = _inproj_weights(w_in)
    s5w = _s5_weights(*ssm_params)
    postw = _post_weights(w_glu, w_out, w_up, w_down)

    pos_p = jnp.arange(T, dtype=I32)
    q, kft, kb, vft, vt, u_tb, kift, ki4, qi4, misc = _inproj(hp, pos_p, g_pre, wm, wih, wil, _largest_tile(T, 512))
    wit = jnp.swapaxes(misc[:, :, :N_IDX_HEADS], 1, 2) * idx_scale
    ao = _prompt_attention(qi4, wit, ki4, q, kb, vt, _largest_tile(T, 256))
    h0 = jnp.zeros((N_SLAB, B, LANES), F32)
    y_tb, hfin = _s5_scan(u_tb, B, B, h0, s5w, d_skip, _largest_tile(T, 128))
    yp = _post(hp, ao, y_tb, postw, b_glu, g_post, g_ffn_pre, g_ffn_post, _largest_tile(T, 256))
    srp, sip = _slabs_to_state(hfin)

    def kv_rows(a, nb, t):
        return jnp.transpose(a.reshape(nb, N_KV_HEADS, HEAD_DIM, t), (0, 3, 1, 2))

    prompt_out = (yp, kv_rows(kft, B, T), kv_rows(vft, B, T), jnp.swapaxes(kift, 1, 2), srp, sip)

    n_s = DB * dec_t
    pos_s = jnp.tile(past_len + jnp.arange(dec_t, dtype=I32), DB)
    q, kft, _, vft, _, u_s, kift, _, qi4, misc = _inproj(hs.reshape(1, n_s, D_MODEL), pos_s, g_pre, wm, wih, wil,
                                                        _largest_tile(n_s, 256))
    qi4 = qi4.reshape(DB, dec_t, N_IDX_HEADS, 4 * IDX_DIM)

    def heads_first(a):
        return jnp.swapaxes(a, 1, 2).reshape(DB, a.shape[2] * dec_t, a.shape[3])

    def per_batch(a):
        return jnp.swapaxes(a.reshape(a.shape[1], DB, dec_t), 0, 1)

    def pad_new(a, dtype):
        return jnp.pad(a, ((0, 0), (0, 0), (0, LANES - dec_t))).astype(dtype)

    k_s, v_s, ki_s = per_batch(kft), per_batch(vft), per_batch(kift)
    qh = heads_first(qi4[..., :IDX_DIM])
    ql = heads_first(qi4[..., 2 * IDX_DIM:3 * IDX_DIM])
    w_s = misc.reshape(DB, dec_t, LANES)[:, :, :N_IDX_HEADS] * idx_scale

    n_phys = cache_k.shape[0]
    pool_ki = jnp.swapaxes(cache_kidx, 1, 2)
    pool_k = jnp.transpose(cache_k, (0, 2, 3, 1)).reshape(n_phys, KV_WIDTH, PAGE_SIZE)
    pool_v = jnp.transpose(cache_v, (0, 2, 3, 1)).reshape(n_phys, KV_WIDTH, PAGE_SIZE)
    n_step = _largest_tile(n_pages, 8)
    sel = _sample_select(page_table, qh, ql, w_s, pad_new(ki_s, F32), pool_ki, n_step)
    qs = heads_first(q.reshape(DB, dec_t, N_HEADS, LANES))
    ao_s = _sample_attention(page_table, sel, qs, pad_new(k_s, BF16), pad_new(v_s, BF16), pool_k, pool_v, n_step)
    ao_s = jnp.swapaxes(ao_s.reshape(DB, N_HEADS, dec_t, LANES), 1, 2).reshape(1, n_s, QPAD).astype(BF16)
    u_t = jnp.swapaxes(u_s.reshape(DB, dec_t, SSM_WIDTH), 0, 1).reshape(n_s, SSM_WIDTH)
    y_t, hfin = _s5_scan(u_t, DB, 1, _state_to_slabs(st_re, st_im), s5w, d_skip, dec_t)
    y_s = jnp.swapaxes(y_t.reshape(dec_t, DB, SSM_WIDTH), 0, 1).reshape(n_s, SSM_WIDTH)
    ys = _post(hs.reshape(1, n_s, D_MODEL), ao_s, y_s, postw, b_glu, g_post, g_ffn_pre, g_ffn_post,
               _largest_tile(n_s, 256))
    srs, sis = _slabs_to_state(hfin)
    sample_out = (ys.reshape(DB, dec_t, D_MODEL), kv_rows(k_s, DB, dec_t), kv_rows(v_s, DB, dec_t),
                  jnp.swapaxes(ki_s, 1, 2), srs, sis)
    return prompt_out, sample_out


def kernel(x_prompt, x_sample, cache_k, cache_v, cache_kidx, state_ssm_re, state_ssm_im, page_table, norm_mix_pre, w_in, lambda_re, lambda_im, log_dt, b_re, b_im, c_re, c_im, d_skip, w_glu, b_glu, w_out, norm_mix_post, norm_ffn_pre, w_up, w_down, norm_ffn_post):
    depth = w_in.shape[0]
    hp, hs = x_prompt, x_sample
    p_acc = [[] for _ in range(5)]
    s_acc = [[] for _ in range(5)]
    for l in range(depth):
        ssm_params = (lambda_re[l], lambda_im[l], log_dt[l], b_re[l], b_im[l], c_re[l], c_im[l])
        p_out, s_out = _layer(hp, hs, cache_k[l], cache_v[l], cache_kidx[l], state_ssm_re[l], state_ssm_im[l],
                              page_table, norm_mix_pre[l], w_in[l], ssm_params, d_skip[l], w_glu[l], b_glu[l],
                              w_out[l], norm_mix_post[l], norm_ffn_pre[l], w_up[l], w_down[l], norm_ffn_post[l])
        hp, hs = p_out[0], s_out[0]
        for acc, vals in ((p_acc, p_out[1:]), (s_acc, s_out[1:])):
            for a, v in zip(acc, vals):
                a.append(v)
    return (hp, hs) + tuple(jnp.stack(a) for a in p_acc) + tuple(jnp.stack(a) for a in s_acc)
```

```python
import functools
import math

import jax
import jax.numpy as jnp
from jax import lax
from jax.experimental import pallas as pl
from jax.experimental.pallas import tpu as pltpu

F32 = jnp.float32
BF16 = jnp.bfloat16
I32 = jnp.int32

D_MODEL = 1024
PAGE_SIZE = 128
HEAD_DIM = 64
ATTN_WIDTH = D_MODEL // 2
N_HEADS = ATTN_WIDTH // HEAD_DIM
N_KV_HEADS = 2
Q_PER_KV = N_HEADS // N_KV_HEADS
KV_WIDTH = N_KV_HEADS * HEAD_DIM
N_IDX_HEADS = 4
IDX_DIM = 64
TOPK_MAX = 256
ROPE_THETA = 10000.0
SSM_WIDTH = D_MODEL - ATTN_WIDTH
SSM_GROUP = 16
N_SSM_GROUPS = SSM_WIDTH // SSM_GROUP
SSM_STATE = 64
D_FF = 4 * D_MODEL
RMS_EPS = 1e-6
NEG = -1e30
OFF_Q = 0
OFF_K = OFF_Q + ATTN_WIDTH
OFF_V = OFF_K + KV_WIDTH
OFF_QI = OFF_V + KV_WIDTH
OFF_KI = OFF_QI + N_IDX_HEADS * IDX_DIM
OFF_WI = OFF_KI + IDX_DIM
OFF_U = OFF_WI + N_IDX_HEADS
IN_WIDTH = OFF_U + SSM_WIDTH

LANES = 128
SUBLANES = 8
INT_MIN = -(2 ** 31)
VMEM_LIMIT = 56 * 1024 * 1024

QPAD = N_HEADS * LANES
MAIN_W = QPAD + 2 * KV_WIDTH + SSM_WIDTH
IDX_W = 2 * N_IDX_HEADS * IDX_DIM + 2 * IDX_DIM + LANES
SSM_LANES = N_SSM_GROUPS * SSM_STATE
N_SLAB = 2 * SSM_LANES // LANES
N_UCHUNK = SSM_WIDTH // LANES
GROUPS_PER_CHUNK = LANES // SSM_GROUP
SLABS_PER_CHUNK = N_SLAB // N_UCHUNK

NT_DIMS = (((1,), (1,)), ((), ()))
LOG2_E = math.log2(math.e)
I16 = jnp.int16
I16_MIN = -(2 ** 15)
I16_MAX = 2 ** 15 - 1


def _cparams(n_grid):
    return pltpu.CompilerParams(dimension_semantics=("arbitrary",) * n_grid,
                                vmem_limit_bytes=VMEM_LIMIT)


def _const_spec(shape):
    zeros = (0,) * len(shape)
    return pl.BlockSpec(shape, lambda *_: zeros, pipeline_mode=pl.Buffered(1))


def _rms(x, g):
    return x * lax.rsqrt(jnp.mean(x * x, axis=-1, keepdims=True) + RMS_EPS) * g


def _split_bf16(x):
    hi = x.astype(BF16)
    lo = (x - hi.astype(F32)).astype(BF16)
    return hi, lo


def _sortable(x):
    b = lax.bitcast_convert_type(x, I32)
    return b ^ ((b >> 31) & jnp.int32(0x7FFFFFFF))


def _rope(x, cos, sin):
    w = x.shape[-1]
    lane = lax.broadcasted_iota(I32, x.shape, 1)
    first = (lane & (HEAD_DIM - 1)) < HEAD_DIM // 2
    sw = jnp.where(first, pltpu.roll(x, w - HEAD_DIM // 2, 1), pltpu.roll(x, HEAD_DIM // 2, 1))
    return x * cos + sw * sin


def _inproj_kernel(x_ref, g_ref, wm_ref, wih_ref, wil_ref, cos_ref, sin_ref,
                   q_ref, kf_ref, kb_ref, vf_ref, vt_ref, u_ref, kif_ref, ki4_ref, qi4_ref, misc_ref):
    xn = _rms(x_ref[...], g_ref[...])
    xh, xl = _split_bf16(xn)
    cos = cos_ref[...]
    sin = sin_ref[...]
    main = jnp.dot(xh, wm_ref[...], preferred_element_type=F32)
    wih = wih_ref[...]
    idx = (jnp.dot(xh, wih, preferred_element_type=F32) + jnp.dot(xl, wih, preferred_element_type=F32)
           + jnp.dot(xh, wil_ref[...], preferred_element_type=F32))

    scale = HEAD_DIM ** -0.5 * LOG2_E
    for h in range(N_HEADS):
        blk = main[:, h * LANES:(h + 1) * LANES]
        q_ref[:, h * LANES:(h + 1) * LANES] = (_rope(blk, cos, sin) * scale).astype(BF16)
    k = _rope(main[:, QPAD:QPAD + KV_WIDTH], cos, sin)
    kf_ref[...] = k.T
    kb_ref[...] = k.astype(BF16)
    vt = main[:, QPAD + KV_WIDTH:QPAD + 2 * KV_WIDTH].T
    vf_ref[...] = vt
    vt_ref[...] = vt.astype(BF16)
    u_ref[...] = main[:, QPAD + 2 * KV_WIDTH:]

    for h in range(N_IDX_HEADS):
        d = _rope(idx[:, h * LANES:(h + 1) * LANES], cos, sin)
        hi, lo = _split_bf16(d)
        qi4_ref[:, 2 * h * LANES:(2 * h + 1) * LANES] = hi
        qi4_ref[:, (2 * h + 1) * LANES:(2 * h + 2) * LANES] = lo
    off = N_IDX_HEADS * LANES
    d = _rope(idx[:, off:off + LANES], cos, sin)
    kif_ref[...] = d.T[:IDX_DIM, :]
    hi = d.astype(BF16).astype(F32)
    lane = lax.broadcasted_iota(I32, d.shape, 1)
    hl = jnp.where(lane < IDX_DIM, hi, d - hi).astype(BF16)
    ki4_ref[:, :LANES] = hl
    ki4_ref[:, LANES:] = hl
    misc_ref[...] = idx[:, off + LANES:]


def _inproj(x, pos, g_pre, wm, wih, wil, tm):
    B, T, _ = x.shape
    half = HEAD_DIM // 2
    inv = ROPE_THETA ** (-jnp.arange(half, dtype=F32) / half)
    ang = pos.astype(F32)[:, None] * inv[None, :]
    cos = jnp.tile(jnp.cos(ang), (1, LANES // half))
    s = jnp.sin(ang)
    sin = jnp.tile(jnp.concatenate([-s, s], axis=1), (1, LANES // HEAD_DIM))

    def row(w):
        return pl.BlockSpec((None, tm, w), lambda b, t: (b, t, 0))

    def tab():
        return pl.BlockSpec((tm, LANES), lambda b, t: (t, 0))

    def col(w):
        return pl.BlockSpec((None, w, tm), lambda b, t: (b, 0, t))

    out_shapes = (
        jax.ShapeDtypeStruct((B, T, QPAD), BF16),
        jax.ShapeDtypeStruct((B, KV_WIDTH, T), F32),
        jax.ShapeDtypeStruct((B, T, KV_WIDTH), BF16),
        jax.ShapeDtypeStruct((B, KV_WIDTH, T), F32),
        jax.ShapeDtypeStruct((B, KV_WIDTH, T), BF16),
        jax.ShapeDtypeStruct((T, B * SSM_WIDTH), F32),
        jax.ShapeDtypeStruct((B, IDX_DIM, T), F32),
        jax.ShapeDtypeStruct((B, T, 4 * IDX_DIM), BF16),
        jax.ShapeDtypeStruct((B, T, 4 * IDX_DIM * N_IDX_HEADS), BF16),
        jax.ShapeDtypeStruct((B, T, LANES), F32),
    )
    out_specs = (
        row(QPAD), col(KV_WIDTH), row(KV_WIDTH), col(KV_WIDTH), col(KV_WIDTH),
        pl.BlockSpec((tm, SSM_WIDTH), lambda b, t: (t, b)),
        col(IDX_DIM), row(4 * IDX_DIM), row(4 * IDX_DIM * N_IDX_HEADS), row(LANES),
    )
    return pl.pallas_call(
        _inproj_kernel,
        grid=(B, T // tm),
        in_specs=[row(D_MODEL), _const_spec((1, D_MODEL)), _const_spec((D_MODEL, MAIN_W)),
                  _const_spec((D_MODEL, IDX_W)), _const_spec((D_MODEL, IDX_W)), tab(), tab()],
        out_specs=out_specs,
        out_shape=out_shapes,
        compiler_params=_cparams(2),
        name="inproj",
    )(x, g_pre.reshape(1, D_MODEL), wm, wih, wil, cos, sin)


def _inproj_weights(w_in):
    cols = []
    zero = jnp.zeros((D_MODEL, HEAD_DIM), F32)
    for h in range(N_HEADS):
        wq = w_in[:, OFF_Q + h * HEAD_DIM:OFF_Q + (h + 1) * HEAD_DIM]
        cols += [wq, zero] if h // Q_PER_KV == 0 else [zero, wq]
    cols += [w_in[:, OFF_K:OFF_QI], w_in[:, OFF_U:]]
    wm = jnp.concatenate(cols, axis=1).astype(BF16)
    icols = []
    for h in range(N_IDX_HEADS):
        wq = w_in[:, OFF_QI + h * IDX_DIM:OFF_QI + (h + 1) * IDX_DIM]
        icols += [wq, wq]
    wk = w_in[:, OFF_KI:OFF_WI]
    icols += [wk, wk, w_in[:, OFF_WI:OFF_U], jnp.zeros((D_MODEL, LANES - N_IDX_HEADS), F32)]
    wi = jnp.concatenate(icols, axis=1)
    wih = wi.astype(BF16)
    wil = (wi - wih.astype(F32)).astype(BF16)
    return wm, wih, wil


def _attn_kernel(qi4_ref, wit_ref, ki4_ref, q_ref, kb_ref, vt_ref, o_ref,
                 keys_ref, khi_ref, klo_ref, qs_ref, lg_ref, bias_ref, p_ref, alpha_ref, tie_ref,
                 m_ref, l_ref, acc_ref, *, topk, tq):
    i = pl.program_id(1)
    nk = i + 1
    w = wit_ref[...]

    def chunk(c):
        return pl.ds(pl.multiple_of(c * tq, tq), tq)

    def score_chunk(c, diagonal):
        kc = ki4_ref[chunk(c), :]
        sc = None
        for h in range(N_IDX_HEADS):
            s = lax.dot_general(kc, qi4_ref[:, 4 * IDX_DIM * h:4 * IDX_DIM * (h + 1)], NT_DIMS,
                                preferred_element_type=F32)
            t = jnp.maximum(s, 0.0) * w[h:h + 1, :]
            sc = t if sc is None else sc + t
        key = _sortable(sc)
        if diagonal:
            kpos = lax.broadcasted_iota(I32, key.shape, 0)
            qpos = lax.broadcasted_iota(I32, key.shape, 1)
            key = jnp.where(kpos <= qpos, key, INT_MIN)
        keys_ref[chunk(c), :] = key
        khi_ref[chunk(c), :] = (key >> 16).astype(I16)

    def score_pair(cc, _):
        score_chunk(2 * cc, False)
        score_chunk(2 * cc + 1, False)
        return 0

    lax.fori_loop(0, i // 2, score_pair, 0)

    @pl.when(i % 2 == 1)
    def _():
        score_chunk(i - 1, False)

    score_chunk(i, True)

    pack_rows = 2 * SUBLANES
    n_acc = 4

    def count16(ref, cand):
        cand = cand.astype(I16)

        def body(c, accs):
            m = jnp.where(ref[chunk(c), :] >= cand, jnp.ones((), I16), jnp.zeros((), I16))
            m = m.reshape(tq // pack_rows, pack_rows, tq)
            accs = list(accs)
            for r in range(tq // pack_rows):
                accs[r % n_acc] = accs[r % n_acc] + m[r]
            return tuple(accs)

        accs = lax.fori_loop(0, nk, body, tuple(jnp.zeros((pack_rows, tq), I16) for _ in range(n_acc)))
        tot = (accs[0] + accs[1]) + (accs[2] + accs[3])
        return jnp.sum(tot.astype(I32), axis=0, keepdims=True)

    def search16(ref):
        def bit_body(b, t):
            cand = t + (jnp.int32(1) << (15 - b))
            return jnp.where(count16(ref, cand) >= topk, cand, t)
        return lax.fori_loop(0, 16, bit_body, jnp.full((1, tq), I16_MIN, I32))

    tau_hi = search16(khi_ref)

    def lo_body(c, _):
        k = keys_ref[chunk(c), :]
        hi = k >> 16
        lo = (k & 0xFFFF) + I16_MIN
        lo = jnp.where(hi == tau_hi, lo, jnp.where(hi > tau_hi, I16_MAX, I16_MIN))
        klo_ref[chunk(c), :] = lo.astype(I16)
        return 0

    lax.fori_loop(0, nk, lo_body, 0)
    tau_lo = search16(klo_ref)
    tau = (tau_hi << 16) + (tau_lo - I16_MIN)
    tau = jnp.maximum(tau, INT_MIN + 1)

    def gt_body(c, acc):
        m = jnp.where(keys_ref[chunk(c), :] > tau, 1, 0).astype(I32)
        return acc + jnp.sum(m.reshape(tq // SUBLANES, SUBLANES, tq), axis=0)

    n_gt = jnp.sum(lax.fori_loop(0, nk, gt_body, jnp.zeros((SUBLANES, tq), I32)), axis=0, keepdims=True)
    quota = (topk - n_gt).astype(F32)

    for h in range(N_HEADS):
        qs_ref[h * tq:(h + 1) * tq, :] = q_ref[:, h * LANES:(h + 1) * LANES]
    m_ref[...] = jnp.full(m_ref.shape, NEG, F32)
    l_ref[...] = jnp.zeros(l_ref.shape, F32)
    acc_ref[...] = jnp.zeros(acc_ref.shape, F32)
    r_i = lax.broadcasted_iota(I32, (tq, tq), 0)
    c_i = lax.broadcasted_iota(I32, (tq, tq), 1)
    ltri = jnp.where(c_i <= r_i, 1.0, 0.0).astype(BF16)

    def prepare(c, slot):
        cc = jnp.minimum(c, nk - 1)
        kc = keys_ref[chunk(cc), :]
        eq = kc == tau
        e = jnp.where(eq, 1.0, 0.0)
        pre = jnp.dot(ltri, e.astype(BF16), preferred_element_type=F32)
        rank = tie_ref[...] + pre - e
        sel = ((kc > tau) | (eq & (rank < quota))) & (c < nk)
        tie_ref[...] = tie_ref[...] + pre[tq - 1:tq, :]
        bias_ref[slot] = jnp.where(sel, 0.0, NEG)
        lg_ref[slot] = lax.dot_general(kb_ref[chunk(cc), :], qs_ref[...], NT_DIMS, preferred_element_type=F32)

    def softmax(slot):
        for j in range(N_HEADS * tq // LANES):
            cols = slice(j * LANES, (j + 1) * LANES)
            jb = j % (tq // LANES)
            x = lg_ref[slot, :, cols] + bias_ref[slot, :, jb * LANES:(jb + 1) * LANES]
            m_prev = m_ref[:, cols]
            m_new = jnp.maximum(m_prev, jnp.max(x, axis=0, keepdims=True))
            alpha = jnp.exp2(m_prev - m_new)
            p = jnp.exp2(x - m_new)
            l_ref[:, cols] = alpha * l_ref[:, cols] + jnp.sum(p, axis=0, keepdims=True)
            m_ref[:, cols] = m_new
            p_ref[slot, :, cols] = p.astype(BF16)
            alpha_ref[slot, :, cols] = alpha

    def value(c, slot):
        cc = jnp.clip(c, 0, nk - 1)
        acc_ref[...] = alpha_ref[slot] * acc_ref[...] + jnp.dot(vt_ref[:, chunk(cc)], p_ref[slot],
                                                                preferred_element_type=F32)

    tie_ref[...] = jnp.zeros(tie_ref.shape, F32)
    p_ref[1] = jnp.zeros(p_ref.shape[1:], BF16)
    alpha_ref[1] = jnp.ones(alpha_ref.shape[1:], F32)
    prepare(0, 0)

    def pair_body(cc, _):
        for slot in range(2):
            c = 2 * cc + slot
            prepare(c + 1, 1 - slot)
            softmax(slot)
            value(c - 1, 1 - slot)
        return 0

    n_pairs = (nk + 1) // 2
    lax.fori_loop(0, n_pairs, pair_body, 0)
    value(2 * n_pairs - 1, 1)
    out = acc_ref[...] / l_ref[...]
    for h in range(N_HEADS):
        o_ref[:, h * LANES:(h + 1) * LANES] = out[:, h * tq:(h + 1) * tq].T.astype(BF16)


def _prompt_attention(qi4, wit, ki4, q, kb, vt, tq):
    B, T, _ = q.shape
    topk = min(TOPK_MAX, T // 4)

    def tile(w):
        return pl.BlockSpec((None, tq, w), lambda b, i: (b, i, 0))

    def full(w):
        return pl.BlockSpec((None, T, w), lambda b, i: (b, 0, 0))

    return pl.pallas_call(
        functools.partial(_attn_kernel, topk=topk, tq=tq),
        grid=(B, T // tq),
        in_specs=[tile(4 * IDX_DIM * N_IDX_HEADS),
                  pl.BlockSpec((None, N_IDX_HEADS, tq), lambda b, i: (b, 0, i)),
                  full(4 * IDX_DIM), tile(QPAD), full(KV_WIDTH),
                  pl.BlockSpec((None, KV_WIDTH, T), lambda b, i: (b, 0, 0))],
        out_specs=tile(QPAD),
        out_shape=jax.ShapeDtypeStruct((B, T, QPAD), BF16),
        scratch_shapes=[pltpu.VMEM((T, tq), I32), pltpu.VMEM((T, tq), I16), pltpu.VMEM((T, tq), I16),
                        pltpu.VMEM((N_HEADS * tq, LANES), BF16), pltpu.VMEM((2, tq, N_HEADS * tq), F32),
                        pltpu.VMEM((2, tq, tq), F32), pltpu.VMEM((2, tq, N_HEADS * tq), BF16),
                        pltpu.VMEM((2, 1, N_HEADS * tq), F32), pltpu.VMEM((1, tq), F32),
                        pltpu.VMEM((1, N_HEADS * tq), F32), pltpu.VMEM((1, N_HEADS * tq), F32),
                        pltpu.VMEM((LANES, N_HEADS * tq), F32)],
        compiler_params=_cparams(2),
        name="prompt_attention",
    )(qi4, wit, ki4, q, kb, vt)


def _s5_prep_kernel(lr_ref, li_ref, ldt_ref, brt_ref, bit_ref, ar_ref, ai_ref, bbr_ref, bbi_ref):
    lr = lr_ref[...]
    li = li_ref[...]
    dt = jnp.exp(ldt_ref[...])
    mag = jnp.exp(lr * dt)
    ar = mag * jnp.cos(li * dt)
    ai = mag * jnp.sin(li * dt)
    den = lr * lr + li * li
    cr = ((ar - 1.0) * lr + ai * li) / den
    ci = (ai * lr - (ar - 1.0) * li) / den
    ar_ref[...] = ar
    ai_ref[...] = ai
    brt = brt_ref[...]
    bit = bit_ref[...]
    bbr_ref[...] = cr[:, None, :] * brt - ci[:, None, :] * bit
    bbi_ref[...] = cr[:, None, :] * bit + ci[:, None, :] * brt


def _s5_weights(lambda_re, lambda_im, log_dt, b_re, b_im, c_re, c_im):
    G, P, C = N_SSM_GROUPS, SSM_STATE, SSM_GROUP
    ar, ai, bbr, bbi = pl.pallas_call(
        _s5_prep_kernel,
        out_shape=(jax.ShapeDtypeStruct((G, P), F32), jax.ShapeDtypeStruct((G, P), F32),
                   jax.ShapeDtypeStruct((G, C, P), F32), jax.ShapeDtypeStruct((G, C, P), F32)),
        name="s5_prep",
    )(lambda_re, lambda_im, log_dt.reshape(G, 1), jnp.swapaxes(b_re, 1, 2), jnp.swapaxes(b_im, 1, 2))
    eye = jnp.eye(GROUPS_PER_CHUNK, dtype=F32)

    def in_blocks(bb):
        x = bb.reshape(N_UCHUNK, GROUPS_PER_CHUNK, C, P)
        x = x[:, :, :, None, :] * eye[None, :, None, :, None]
        return x.reshape(N_UCHUNK, LANES, GROUPS_PER_CHUNK * P)

    def out_blocks(cc):
        x = jnp.swapaxes(cc, 1, 2).reshape(N_UCHUNK, GROUPS_PER_CHUNK, P, C)
        x = x[:, :, :, None, :] * eye[None, :, None, :, None]
        return x.reshape(N_UCHUNK, GROUPS_PER_CHUNK * P, LANES)

    bblk = jnp.concatenate([in_blocks(bbr), in_blocks(bbi)], axis=2).astype(BF16)
    cblk = jnp.concatenate([out_blocks(c_re), out_blocks(-c_im)], axis=1).astype(BF16)
    half = SLABS_PER_CHUNK // 2

    def slabs(a):
        x = a.reshape(N_UCHUNK, 1, half, 1, LANES)
        return jnp.broadcast_to(x, (N_UCHUNK, 2, half, 1, LANES)).reshape(N_SLAB, 1, LANES)

    return slabs(ar), slabs(ai), bblk, cblk


def _state_to_slabs(h_re, h_im):
    B = h_re.shape[0]
    half = SLABS_PER_CHUNK // 2
    re = h_re.reshape(B, N_UCHUNK, 1, half, LANES)
    im = h_im.reshape(B, N_UCHUNK, 1, half, LANES)
    x = jnp.concatenate([re, im], axis=2).reshape(B, N_SLAB, LANES)
    return jnp.swapaxes(x, 0, 1)


def _slabs_to_state(h):
    B = h.shape[1]
    half = SLABS_PER_CHUNK // 2
    x = jnp.swapaxes(h, 0, 1).reshape(B, N_UCHUNK, 2, half * LANES)
    re = x[:, :, 0].reshape(B, N_SSM_GROUPS, SSM_STATE)
    im = x[:, :, 1].reshape(B, N_SSM_GROUPS, SSM_STATE)
    return re, im


def _s5_kernel(u_ref, h0_ref, ar_ref, ai_ref, bblk_ref, cblk_ref, d_ref, y_ref, hout_ref, bu_ref, hs_ref,
               *, nb, tj, lane_batches):
    rows = tj * nb
    half = SLABS_PER_CHUNK // 2
    blk_rows = rows // lane_batches

    @pl.when(pl.program_id(0) == 0)
    def _():
        hout_ref[...] = h0_ref[...]

    def batch_rows(lb):
        if lane_batches == 1:
            return pl.ds(0, blk_rows)
        return pl.ds(lb, blk_rows, stride=lane_batches)

    for lb in range(lane_batches):
        for q in range(N_UCHUNK):
            cols = slice(lb * SSM_WIDTH + q * LANES, lb * SSM_WIDTH + (q + 1) * LANES)
            bu = jnp.dot(u_ref[:, cols].astype(BF16), bblk_ref[q], preferred_element_type=F32)
            for s in range(SLABS_PER_CHUNK):
                bu_ref[q * SLABS_PER_CHUNK + s, batch_rows(lb), :] = bu[:, s * LANES:(s + 1) * LANES]

    def step(j, h):
        cur = pl.ds(j * nb, nb)
        new = list(h)
        for q in range(N_UCHUNK):
            for r in range(half):
                sr = q * SLABS_PER_CHUNK + r
                si = sr + half
                a_re = ar_ref[sr]
                a_im = ai_ref[sr]
                new[sr] = a_re * h[sr] - a_im * h[si] + bu_ref[sr, cur, :]
                new[si] = a_re * h[si] + a_im * h[sr] + bu_ref[si, cur, :]
                hs_ref[sr, cur, :] = new[sr]
                hs_ref[si, cur, :] = new[si]
        return tuple(new)

    h = lax.fori_loop(0, tj, step, tuple(hout_ref[s] for s in range(N_SLAB)))
    for s in range(N_SLAB):
        hout_ref[s] = h[s]

    for lb in range(lane_batches):
        for q in range(N_UCHUNK):
            hq = jnp.concatenate([hs_ref[q * SLABS_PER_CHUNK + s, batch_rows(lb), :]
                                  for s in range(SLABS_PER_CHUNK)], axis=1).astype(BF16)
            yq = jnp.dot(hq, cblk_ref[q], preferred_element_type=F32)
            cols = slice(lb * SSM_WIDTH + q * LANES, lb * SSM_WIDTH + (q + 1) * LANES)
            y_ref[:, cols] = yq + d_ref[:, q * LANES:(q + 1) * LANES] * u_ref[:, cols]


def _s5_scan(u, nb, lane_batches, h0_slabs, weights, d_skip, tj):
    T = u.shape[0] * lane_batches // nb
    ar, ai, bblk, cblk = weights
    rows = tj * nb
    blk = (rows // lane_batches, lane_batches * SSM_WIDTH)
    y, hout = pl.pallas_call(
        functools.partial(_s5_kernel, nb=nb, tj=tj, lane_batches=lane_batches),
        grid=(T // tj,),
        in_specs=[pl.BlockSpec(blk, lambda t: (t, 0)),
                  _const_spec((N_SLAB, nb, LANES)), _const_spec((N_SLAB, 1, LANES)),
                  _const_spec((N_SLAB, 1, LANES)),
                  _const_spec((N_UCHUNK, LANES, SLABS_PER_CHUNK * LANES)),
                  _const_spec((N_UCHUNK, SLABS_PER_CHUNK * LANES, LANES)),
                  _const_spec((1, SSM_WIDTH))],
        out_specs=(pl.BlockSpec(blk, lambda t: (t, 0)),
                   pl.BlockSpec((N_SLAB, nb, LANES), lambda t: (0, 0, 0))),
        out_shape=(jax.ShapeDtypeStruct(u.shape, F32),
                   jax.ShapeDtypeStruct((N_SLAB, nb, LANES), F32)),
        scratch_shapes=[pltpu.VMEM((N_SLAB, rows, LANES), F32), pltpu.VMEM((N_SLAB, rows, LANES), F32)],
        compiler_params=_cparams(1),
        name="s5_scan",
    )(u, h0_slabs, ar, ai, bblk, cblk, d_skip.reshape(1, SSM_WIDTH))
    return y, hout


def _post_kernel(x_ref, ao_ref, y_ref, wglu_ref, bglu_ref, woa_ref, wos_ref, gpost_ref, gpre_ref,
                 wup_ref, wdn_ref, gfpost_ref, o_ref):
    z = jax.nn.gelu(y_ref[...])
    gate = jax.nn.sigmoid(jnp.dot(z.astype(BF16), wglu_ref[...], preferred_element_type=F32) + bglu_ref[...])
    so = z * gate
    mix = (jnp.dot(ao_ref[...], woa_ref[...], preferred_element_type=F32)
           + jnp.dot(so.astype(BF16), wos_ref[...], preferred_element_type=F32))
    x = x_ref[...] + _rms(mix, gpost_ref[...])
    hdn = jnp.dot(_rms(x, gpre_ref[...]).astype(BF16), wup_ref[...], preferred_element_type=F32)
    hdn = jnp.square(jnp.maximum(hdn, 0.0))
    dn = jnp.dot(hdn.astype(BF16), wdn_ref[...], preferred_element_type=F32)
    o_ref[...] = x + _rms(dn, gfpost_ref[...])


def _post_weights(w_glu, w_out, w_up, w_down):
    zero = jnp.zeros((HEAD_DIM, D_MODEL), F32)
    rows = []
    for h in range(N_HEADS):
        wh = w_out[h * HEAD_DIM:(h + 1) * HEAD_DIM]
        rows += [wh, zero] if h // Q_PER_KV == 0 else [zero, wh]
    woa = jnp.concatenate(rows, axis=0).astype(BF16)
    return (w_glu.astype(BF16), woa, w_out[ATTN_WIDTH:].astype(BF16), w_up.astype(BF16), w_down.astype(BF16))


def _post(x, ao, y_tb, weights, b_glu, g_post, g_ffn_pre, g_ffn_post, tm):
    B, T, _ = x.shape
    wglu, woa, wos, wup, wdn = weights

    def row(w):
        return pl.BlockSpec((None, tm, w), lambda b, t: (b, t, 0))

    def vec(a):
        return a.reshape(1, -1)

    return pl.pallas_call(
        _post_kernel,
        grid=(B, T // tm),
        in_specs=[row(D_MODEL), row(QPAD), pl.BlockSpec((tm, SSM_WIDTH), lambda b, t: (t, b)),
                  _const_spec((SSM_WIDTH, SSM_WIDTH)), _const_spec((1, SSM_WIDTH)),
                  _const_spec((QPAD, D_MODEL)), _const_spec((SSM_WIDTH, D_MODEL)),
                  _const_spec((1, D_MODEL)), _const_spec((1, D_MODEL)),
                  _const_spec((D_MODEL, D_FF)), _const_spec((D_FF, D_MODEL)), _const_spec((1, D_MODEL))],
        out_specs=row(D_MODEL),
        out_shape=jax.ShapeDtypeStruct((B, T, D_MODEL), F32),
        compiler_params=_cparams(2),
        name="post",
    )(x, ao, y_tb, wglu, vec(b_glu), woa, wos, vec(g_post), vec(g_ffn_pre), wup, wdn, vec(g_ffn_post))


def _page_map(b, j, pt, *, r, n):
    return (pt[b, j * n + r], 0, 0)


def _skeys_kernel(pt_ref, qh_ref, ql_ref, w_ref, kn_ref, *rest, n_pages_step, dec_t):
    page_refs = rest[:n_pages_step]
    keys_ref, knew_ref = rest[n_pages_step:]
    j = pl.program_id(1)
    qh = qh_ref[...]
    ql = ql_ref[...]
    w = w_ref[...]

    def scores(kc):
        kh, kl = _split_bf16(kc)
        s = (jnp.dot(qh, kh, preferred_element_type=F32) + jnp.dot(ql, kh, preferred_element_type=F32)
             + jnp.dot(qh, kl, preferred_element_type=F32))
        sc = None
        for h in range(N_IDX_HEADS):
            t = jnp.maximum(s[h * dec_t:(h + 1) * dec_t], 0.0) * w[:, h:h + 1]
            sc = t if sc is None else sc + t
        return _sortable(sc)

    keys_ref[...] = scores(jnp.concatenate([r[...] for r in page_refs], axis=1))

    @pl.when(j == pl.num_programs(1) - 1)
    def _():
        kn = scores(kn_ref[...])
        qi_ = lax.broadcasted_iota(I32, kn.shape, 0)
        ki_ = lax.broadcasted_iota(I32, kn.shape, 1)
        knew_ref[...] = jnp.where(ki_ <= qi_, kn, INT_MIN)


def _ssearch_kernel(kp_ref, kn_ref, sel_ref, *, topk, past_len):
    def count(pred_past, pred_new):
        c = jnp.sum(jnp.where(pred_past, 1, 0).astype(I32), axis=1, keepdims=True)
        return c + jnp.sum(jnp.where(pred_new, 1, 0).astype(I32), axis=1, keepdims=True)

    def count_ge(cand):
        return count(kp_ref[...] >= cand, kn_ref[...] >= cand)

    rows = kp_ref.shape[0]
    tau = jnp.where(count_ge(jnp.zeros((rows, 1), I32)) >= topk, 0, INT_MIN).astype(I32)

    def bit_body(b, tau):
        cand = tau + (jnp.int32(1) << (30 - b))
        return jnp.where(count_ge(cand) >= topk, cand, tau)

    tau = lax.fori_loop(0, 31, bit_body, tau)
    tau = jnp.maximum(tau, INT_MIN + 1)
    quota = topk - count(kp_ref[...] > tau, kn_ref[...] > tau)
    pos_p = lax.broadcasted_iota(I32, kp_ref.shape, 1)
    pos_n = past_len + lax.broadcasted_iota(I32, kn_ref.shape, 1)
    n_bits = max(1, (past_len + kn_ref.shape[1] - 1).bit_length())

    def idx_body(b, lim):
        cand = lim + (jnp.int32(1) << (n_bits - 1 - b))
        ties = count((kp_ref[...] == tau) & (pos_p < cand), (kn_ref[...] == tau) & (pos_n < cand))
        return jnp.where(ties < quota, cand, lim)

    lim = lax.fori_loop(0, n_bits, idx_body, jnp.zeros_like(tau))
    kp = kp_ref[...]
    kn = kn_ref[...]
    sel_ref[:, :past_len] = jnp.where((kp > tau) | ((kp == tau) & (pos_p <= lim)), 1.0, 0.0)
    sel_ref[:, past_len:] = jnp.where((kn > tau) | ((kn == tau) & (pos_n <= lim)), 1.0, 0.0)


def _sample_select(page_table, qh, ql, w, ki_new_pad, cache_kidx, n_pages_step):
    DB, n_pages = page_table.shape
    dec_t = w.shape[1]
    past_len = n_pages * PAGE_SIZE
    lp = past_len + LANES
    topk = min(TOPK_MAX, (past_len + dec_t) // 4)
    rows = N_IDX_HEADS * dec_t
    step_keys = n_pages_step * PAGE_SIZE

    def fixed(shape):
        return pl.BlockSpec((None,) + shape, lambda b, j, pt: (b, 0, 0))

    pages = [pl.BlockSpec((None, IDX_DIM, PAGE_SIZE), functools.partial(_page_map, r=r, n=n_pages_step))
             for r in range(n_pages_step)]
    grid_spec = pltpu.PrefetchScalarGridSpec(
        num_scalar_prefetch=1,
        grid=(DB, n_pages // n_pages_step),
        in_specs=[fixed((rows, IDX_DIM)), fixed((rows, IDX_DIM)), fixed((dec_t, N_IDX_HEADS)),
                  fixed((IDX_DIM, LANES))] + pages,
        out_specs=(pl.BlockSpec((None, dec_t, step_keys), lambda b, j, pt: (b, 0, j)), fixed((dec_t, LANES))),
    )
    keys_past, keys_new = pl.pallas_call(
        functools.partial(_skeys_kernel, n_pages_step=n_pages_step, dec_t=dec_t),
        grid_spec=grid_spec,
        out_shape=(jax.ShapeDtypeStruct((DB, dec_t, past_len), I32), jax.ShapeDtypeStruct((DB, dec_t, LANES), I32)),
        compiler_params=_cparams(2),
        name="sample_keys",
    )(page_table, qh, ql, w, ki_new_pad, *([cache_kidx] * n_pages_step))
    sel = pl.pallas_call(
        functools.partial(_ssearch_kernel, topk=topk, past_len=past_len),
        out_shape=jax.ShapeDtypeStruct((DB * dec_t, lp), F32),
        compiler_params=pltpu.CompilerParams(vmem_limit_bytes=VMEM_LIMIT),
        name="sample_search",
    )(keys_past.reshape(DB * dec_t, past_len), keys_new.reshape(DB * dec_t, LANES))
    return sel.reshape(DB, dec_t, lp)


def _sattn_kernel(pt_ref, sel_ref, q_ref, kn_ref, vn_ref, *rest, n_pages_step, past_len, dec_t):
    k_refs = rest[:n_pages_step]
    v_refs = rest[n_pages_step:2 * n_pages_step]
    o_ref, m_ref, l_ref, acc_ref = rest[2 * n_pages_step:]
    j = pl.program_id(1)
    step_keys = n_pages_step * PAGE_SIZE
    q = q_ref[...]

    @pl.when(j == 0)
    def _():
        m_ref[...] = jnp.full(m_ref.shape, NEG, F32)
        l_ref[...] = jnp.zeros(l_ref.shape, F32)
        acc_ref[...] = jnp.zeros(acc_ref.shape, F32)

    def update(kc, vc, sel):
        n = kc.shape[1]
        lg = jnp.dot(q, kc, preferred_element_type=F32)
        lg = jnp.where(sel[None] > 0.5, lg.reshape(N_HEADS, dec_t, n), NEG).reshape(N_HEADS * dec_t, n)
        m_prev = m_ref[...]
        m_new = jnp.maximum(m_prev, jnp.max(lg, axis=1, keepdims=True))
        alpha = jnp.exp2(m_prev - m_new)
        p = jnp.exp2(lg - m_new)
        l_ref[...] = alpha * l_ref[...] + jnp.sum(p, axis=1, keepdims=True)
        acc_ref[...] = alpha * acc_ref[...] + lax.dot_general(p.astype(BF16), vc, NT_DIMS,
                                                              preferred_element_type=F32)
        m_ref[...] = m_new

    kc = jnp.concatenate([r[...] for r in k_refs], axis=1).astype(BF16)
    vc = jnp.concatenate([r[...] for r in v_refs], axis=1).astype(BF16)
    update(kc, vc, sel_ref[:, pl.ds(pl.multiple_of(j * step_keys, step_keys), step_keys)])

    @pl.when(j == pl.num_programs(1) - 1)
    def _():
        update(kn_ref[...], vn_ref[...], sel_ref[:, past_len:])
        o_ref[...] = acc_ref[...] / l_ref[...]


def _sample_attention(page_table, sel, q, k_new_pad, v_new_pad, cache_k, cache_v, n_pages_step):
    DB, n_pages = page_table.shape
    dec_t = sel.shape[1]
    past_len = n_pages * PAGE_SIZE
    lp = past_len + LANES
    rows = N_HEADS * dec_t

    def fixed(shape):
        return pl.BlockSpec((None,) + shape, lambda b, j, pt: (b, 0, 0))

    def pages():
        return [pl.BlockSpec((None, KV_WIDTH, PAGE_SIZE), functools.partial(_page_map, r=r, n=n_pages_step))
                for r in range(n_pages_step)]

    grid_spec = pltpu.PrefetchScalarGridSpec(
        num_scalar_prefetch=1,
        grid=(DB, n_pages // n_pages_step),
        in_specs=[fixed((dec_t, lp)), fixed((rows, KV_WIDTH)), fixed((KV_WIDTH, LANES)),
                  fixed((KV_WIDTH, LANES))] + pages() + pages(),
        out_specs=fixed((rows, KV_WIDTH)),
        scratch_shapes=[pltpu.VMEM((rows, 1), F32), pltpu.VMEM((rows, 1), F32),
                        pltpu.VMEM((rows, KV_WIDTH), F32)],
    )
    return pl.pallas_call(
        functools.partial(_sattn_kernel, n_pages_step=n_pages_step, past_len=past_len, dec_t=dec_t),
        grid_spec=grid_spec,
        out_shape=jax.ShapeDtypeStruct((DB, rows, KV_WIDTH), F32),
        compiler_params=_cparams(2),
        name="sample_attention",
    )(page_table, sel, q, k_new_pad, v_new_pad, *([cache_k] * n_pages_step), *([cache_v] * n_pages_step))


def _largest_tile(n, cap):
    t = min(n, cap)
    while n % t:
        t //= 2
    return t


def _layer(hp, hs, cache_k, cache_v, cache_kidx, st_re, st_im, page_table, g_pre, w_in, ssm_params, d_skip,
           w_glu, b_glu, w_out, g_post, g_ffn_pre, w_up, w_down, g_ffn_post):
    B, T, _ = hp.shape
    DB, dec_t, _ = hs.shape
    n_pages = page_table.shape[1]
    past_len = n_pages * PAGE_SIZE
    idx_scale = (IDX_DIM ** -0.5) * (N_IDX_HEADS ** -0.5)

    wm, wih, wil = _inproj_weights(w_in)
    s5w = _s5_weights(*ssm_params)
    postw = _post_weights(w_glu, w_out, w_up, w_down)

    pos_p = jnp.arange(T, dtype=I32)
    q, kft, kb, vft, vt, u_tb, kift, ki4, qi4, misc = _inproj(hp, pos_p, g_pre, wm, wih, wil, _largest_tile(T, 512))
    wit = jnp.swapaxes(misc[:, :, :N_IDX_HEADS], 1, 2) * idx_scale
    ao = _prompt_attention(qi4, wit, ki4, q, kb, vt, _largest_tile(T, 256))
    h0 = jnp.zeros((N_SLAB, B, LANES), F32)
    y_tb, hfin = _s5_scan(u_tb, B, B, h0, s5w, d_skip, _largest_tile(T, 256))
    yp = _post(hp, ao, y_tb, postw, b_glu, g_post, g_ffn_pre, g_ffn_post, _largest_tile(T, 256))
    srp, sip = _slabs_to_state(hfin)

    def kv_rows(a, nb, t):
        return jnp.transpose(a.reshape(nb, N_KV_HEADS, HEAD_DIM, t), (0, 3, 1, 2))

    prompt_out = (yp, kv_rows(kft, B, T), kv_rows(vft, B, T), jnp.swapaxes(kift, 1, 2), srp, sip)

    n_s = DB * dec_t
    pos_s = jnp.tile(past_len + jnp.arange(dec_t, dtype=I32), DB)
    q, kft, _, vft, _, u_s, kift, _, qi4, misc = _inproj(hs.reshape(1, n_s, D_MODEL), pos_s, g_pre, wm, wih, wil,
                                                        _largest_tile(n_s, 256))
    qi4 = qi4.reshape(DB, dec_t, N_IDX_HEADS, 4 * IDX_DIM)

    def heads_first(a):
        return jnp.swapaxes(a, 1, 2).reshape(DB, a.shape[2] * dec_t, a.shape[3])

    def per_batch(a):
        return jnp.swapaxes(a.reshape(a.shape[1], DB, dec_t), 0, 1)

    def pad_new(a, dtype):
        return jnp.pad(a, ((0, 0), (0, 0), (0, LANES - dec_t))).astype(dtype)

    k_s, v_s, ki_s = per_batch(kft), per_batch(vft), per_batch(kift)
    qh = heads_first(qi4[..., :IDX_DIM])
    ql = heads_first(qi4[..., 2 * IDX_DIM:3 * IDX_DIM])
    w_s = misc.reshape(DB, dec_t, LANES)[:, :, :N_IDX_HEADS] * idx_scale

    n_phys = cache_k.shape[0]
    pool_ki = jnp.swapaxes(cache_kidx, 1, 2)
    pool_k = jnp.transpose(cache_k, (0, 2, 3, 1)).reshape(n_phys, KV_WIDTH, PAGE_SIZE)
    pool_v = jnp.transpose(cache_v, (0, 2, 3, 1)).reshape(n_phys, KV_WIDTH, PAGE_SIZE)
    n_step = _largest_tile(n_pages, 32)
    sel = _sample_select(page_table, qh, ql, w_s, pad_new(ki_s, F32), pool_ki, n_step)
    qs = heads_first(q.reshape(DB, dec_t, N_HEADS, LANES))
    ao_s = _sample_attention(page_table, sel, qs, pad_new(k_s, BF16), pad_new(v_s, BF16), pool_k, pool_v, n_step)
    ao_s = jnp.swapaxes(ao_s.reshape(DB, N_HEADS, dec_t, LANES), 1, 2).reshape(1, n_s, QPAD).astype(BF16)
    u_t = jnp.swapaxes(u_s.reshape(DB, dec_t, SSM_WIDTH), 0, 1).reshape(n_s, SSM_WIDTH)
    y_t, hfin = _s5_scan(u_t, DB, 1, _state_to_slabs(st_re, st_im), s5w, d_skip, dec_t)
    y_s = jnp.swapaxes(y_t.reshape(dec_t, DB, SSM_WIDTH), 0, 1).reshape(n_s, SSM_WIDTH)
    ys = _post(hs.reshape(1, n_s, D_MODEL), ao_s, y_s, postw, b_glu, g_post, g_ffn_pre, g_ffn_post,
               _largest_tile(n_s, 256))
    srs, sis = _slabs_to_state(hfin)
    sample_out = (ys.reshape(DB, dec_t, D_MODEL), kv_rows(k_s, DB, dec_t), kv_rows(v_s, DB, dec_t),
                  jnp.swapaxes(ki_s, 1, 2), srs, sis)
    return prompt_out, sample_out


def kernel(x_prompt, x_sample, cache_k, cache_v, cache_kidx, state_ssm_re, state_ssm_im, page_table, norm_mix_pre, w_in, lambda_re, lambda_im, log_dt, b_re, b_im, c_re, c_im, d_skip, w_glu, b_glu, w_out, norm_mix_post, norm_ffn_pre, w_up, w_down, norm_ffn_post):
    depth = w_in.shape[0]
    hp, hs = x_prompt, x_sample
    p_acc = [[] for _ in range(5)]
    s_acc = [[] for _ in range(5)]
    for l in range(depth):
        ssm_params = (lambda_re[l], lambda_im[l], log_dt[l], b_re[l], b_im[l], c_re[l], c_im[l])
        p_out, s_out = _layer(hp, hs, cache_k[l], cache_v[l], cache_kidx[l], state_ssm_re[l], state_ssm_im[l],
                              page_table, norm_mix_pre[l], w_in[l], ssm_params, d_skip[l], w_glu[l], b_glu[l],
                              w_out[l], norm_mix_post[l], norm_ffn_pre[l], w_up[l], w_down[l], norm_ffn_post[l])
        hp, hs = p_out[0], s_out[0]
        for acc, vals in ((p_acc, p_out[1:]), (s_acc, s_out[1:])):
            for a, v in zip(acc, vals):
                a.append(v)
    return (hp, hs) + tuple(jnp.stack(a) for a in p_acc) + tuple(jnp.stack(a) for a in s_acc)
```

```python
import functools
import math

import jax
import jax.numpy as jnp
from jax import lax
from jax.experimental import pallas as pl
from jax.experimental.pallas import tpu as pltpu

F32 = jnp.float32
BF16 = jnp.bfloat16
I32 = jnp.int32

D_MODEL = 1024
PAGE_SIZE = 128
HEAD_DIM = 64
ATTN_WIDTH = D_MODEL // 2
N_HEADS = ATTN_WIDTH // HEAD_DIM
N_KV_HEADS = 2
Q_PER_KV = N_HEADS // N_KV_HEADS
KV_WIDTH = N_KV_HEADS * HEAD_DIM
N_IDX_HEADS = 4
IDX_DIM = 64
TOPK_MAX = 256
ROPE_THETA = 10000.0
SSM_WIDTH = D_MODEL - ATTN_WIDTH
SSM_GROUP = 16
N_SSM_GROUPS = SSM_WIDTH // SSM_GROUP
SSM_STATE = 64
D_FF = 4 * D_MODEL
RMS_EPS = 1e-6
NEG = -1e30
OFF_Q = 0
OFF_K = OFF_Q + ATTN_WIDTH
OFF_V = OFF_K + KV_WIDTH
OFF_QI = OFF_V + KV_WIDTH
OFF_KI = OFF_QI + N_IDX_HEADS * IDX_DIM
OFF_WI = OFF_KI + IDX_DIM
OFF_U = OFF_WI + N_IDX_HEADS
IN_WIDTH = OFF_U + SSM_WIDTH

LANES = 128
SUBLANES = 8
INT_MIN = -(2 ** 31)
VMEM_LIMIT = 56 * 1024 * 1024

QPAD = N_HEADS * LANES
MAIN_W = QPAD + 2 * KV_WIDTH + SSM_WIDTH
IDX_W = 2 * N_IDX_HEADS * IDX_DIM + 2 * IDX_DIM + LANES
SSM_LANES = N_SSM_GROUPS * SSM_STATE
N_SLAB = 2 * SSM_LANES // LANES
N_UCHUNK = SSM_WIDTH // LANES
GROUPS_PER_CHUNK = LANES // SSM_GROUP
SLABS_PER_CHUNK = N_SLAB // N_UCHUNK

NT_DIMS = (((1,), (1,)), ((), ()))
LOG2_E = math.log2(math.e)
I16 = jnp.int16
I16_MIN = -(2 ** 15)
I16_MAX = 2 ** 15 - 1
VT_ROWS = KV_WIDTH + 2 * SUBLANES


def _cparams(n_grid):
    return pltpu.CompilerParams(dimension_semantics=("arbitrary",) * n_grid,
                                vmem_limit_bytes=VMEM_LIMIT)


def _const_spec(shape):
    zeros = (0,) * len(shape)
    return pl.BlockSpec(shape, lambda *_: zeros, pipeline_mode=pl.Buffered(1))


def _rms(x, g):
    return x * lax.rsqrt(jnp.mean(x * x, axis=-1, keepdims=True) + RMS_EPS) * g


def _split_bf16(x):
    hi = x.astype(BF16)
    lo = (x - hi.astype(F32)).astype(BF16)
    return hi, lo


def _sortable(x):
    b = lax.bitcast_convert_type(x, I32)
    return b ^ ((b >> 31) & jnp.int32(0x7FFFFFFF))


def _rope(x, cos, sin):
    w = x.shape[-1]
    lane = lax.broadcasted_iota(I32, x.shape, 1)
    first = (lane & (HEAD_DIM - 1)) < HEAD_DIM // 2
    sw = jnp.where(first, pltpu.roll(x, w - HEAD_DIM // 2, 1), pltpu.roll(x, HEAD_DIM // 2, 1))
    return x * cos + sw * sin


def _inproj_kernel(x_ref, g_ref, wm_ref, wih_ref, wil_ref, cos_ref, sin_ref,
                   q_ref, kf_ref, kb_ref, vf_ref, vt_ref, u_ref, kif_ref, ki4_ref, qi4_ref, misc_ref):
    xn = _rms(x_ref[...], g_ref[...])
    xh, xl = _split_bf16(xn)
    cos = cos_ref[...]
    sin = sin_ref[...]
    main = jnp.dot(xh, wm_ref[...], preferred_element_type=F32)
    wih = wih_ref[...]
    idx = (jnp.dot(xh, wih, preferred_element_type=F32) + jnp.dot(xl, wih, preferred_element_type=F32)
           + jnp.dot(xh, wil_ref[...], preferred_element_type=F32))

    scale = HEAD_DIM ** -0.5 * LOG2_E
    for h in range(N_HEADS):
        blk = main[:, h * LANES:(h + 1) * LANES]
        q_ref[:, h * LANES:(h + 1) * LANES] = (_rope(blk, cos, sin) * scale).astype(BF16)
    k = _rope(main[:, QPAD:QPAD + KV_WIDTH], cos, sin)
    kf_ref[...] = k.T
    kb_ref[...] = k.astype(BF16)
    vt = main[:, QPAD + KV_WIDTH:QPAD + 2 * KV_WIDTH].T
    vf_ref[...] = vt
    vt_ref[:KV_WIDTH, :] = vt.astype(BF16)
    vt_ref[KV_WIDTH:, :] = jnp.ones((VT_ROWS - KV_WIDTH, vt.shape[1]), BF16)
    u_ref[...] = main[:, QPAD + 2 * KV_WIDTH:]

    for h in range(N_IDX_HEADS):
        d = _rope(idx[:, h * LANES:(h + 1) * LANES], cos, sin)
        hi, lo = _split_bf16(d)
        qi4_ref[:, 2 * h * LANES:(2 * h + 1) * LANES] = hi
        qi4_ref[:, (2 * h + 1) * LANES:(2 * h + 2) * LANES] = lo
    off = N_IDX_HEADS * LANES
    d = _rope(idx[:, off:off + LANES], cos, sin)
    kif_ref[...] = d.T[:IDX_DIM, :]
    hi = d.astype(BF16).astype(F32)
    lane = lax.broadcasted_iota(I32, d.shape, 1)
    hl = jnp.where(lane < IDX_DIM, hi, d - hi).astype(BF16)
    ki4_ref[:, :LANES] = hl
    ki4_ref[:, LANES:] = hl
    misc_ref[...] = idx[:, off + LANES:]


def _inproj(x, pos, g_pre, wm, wih, wil, tm):
    B, T, _ = x.shape
    half = HEAD_DIM // 2
    inv = ROPE_THETA ** (-jnp.arange(half, dtype=F32) / half)
    ang = pos.astype(F32)[:, None] * inv[None, :]
    cos = jnp.tile(jnp.cos(ang), (1, LANES // half))
    s = jnp.sin(ang)
    sin = jnp.tile(jnp.concatenate([-s, s], axis=1), (1, LANES // HEAD_DIM))

    def row(w):
        return pl.BlockSpec((None, tm, w), lambda b, t: (b, t, 0))

    def tab():
        return pl.BlockSpec((tm, LANES), lambda b, t: (t, 0))

    def col(w):
        return pl.BlockSpec((None, w, tm), lambda b, t: (b, 0, t))

    out_shapes = (
        jax.ShapeDtypeStruct((B, T, QPAD), BF16),
        jax.ShapeDtypeStruct((B, KV_WIDTH, T), F32),
        jax.ShapeDtypeStruct((B, T, KV_WIDTH), BF16),
        jax.ShapeDtypeStruct((B, KV_WIDTH, T), F32),
        jax.ShapeDtypeStruct((B, VT_ROWS, T), BF16),
        jax.ShapeDtypeStruct((T, B * SSM_WIDTH), F32),
        jax.ShapeDtypeStruct((B, IDX_DIM, T), F32),
        jax.ShapeDtypeStruct((B, T, 4 * IDX_DIM), BF16),
        jax.ShapeDtypeStruct((B, T, 4 * IDX_DIM * N_IDX_HEADS), BF16),
        jax.ShapeDtypeStruct((B, T, LANES), F32),
    )
    out_specs = (
        row(QPAD), col(KV_WIDTH), row(KV_WIDTH), col(KV_WIDTH), col(VT_ROWS),
        pl.BlockSpec((tm, SSM_WIDTH), lambda b, t: (t, b)),
        col(IDX_DIM), row(4 * IDX_DIM), row(4 * IDX_DIM * N_IDX_HEADS), row(LANES),
    )
    return pl.pallas_call(
        _inproj_kernel,
        grid=(B, T // tm),
        in_specs=[row(D_MODEL), _const_spec((1, D_MODEL)), _const_spec((D_MODEL, MAIN_W)),
                  _const_spec((D_MODEL, IDX_W)), _const_spec((D_MODEL, IDX_W)), tab(), tab()],
        out_specs=out_specs,
        out_shape=out_shapes,
        compiler_params=_cparams(2),
        name="inproj",
    )(x, g_pre.reshape(1, D_MODEL), wm, wih, wil, cos, sin)


def _inproj_weights(w_in):
    cols = []
    zero = jnp.zeros((D_MODEL, HEAD_DIM), F32)
    for h in range(N_HEADS):
        wq = w_in[:, OFF_Q + h * HEAD_DIM:OFF_Q + (h + 1) * HEAD_DIM]
        cols += [wq, zero] if h // Q_PER_KV == 0 else [zero, wq]
    cols += [w_in[:, OFF_K:OFF_QI], w_in[:, OFF_U:]]
    wm = jnp.concatenate(cols, axis=1).astype(BF16)
    icols = []
    for h in range(N_IDX_HEADS):
        wq = w_in[:, OFF_QI + h * IDX_DIM:OFF_QI + (h + 1) * IDX_DIM]
        icols += [wq, wq]
    wk = w_in[:, OFF_KI:OFF_WI]
    icols += [wk, wk, w_in[:, OFF_WI:OFF_U], jnp.zeros((D_MODEL, LANES - N_IDX_HEADS), F32)]
    wi = jnp.concatenate(icols, axis=1)
    wih = wi.astype(BF16)
    wil = (wi - wih.astype(F32)).astype(BF16)
    return wm, wih, wil


def _attn_kernel(qi4_ref, wit_ref, ki4_ref, q_ref, kb_ref, vt_ref, o_ref,
                 keys_ref, khi_ref, klo_ref, qs_ref, lg_ref, p_ref, alpha_ref, tie_ref,
                 m_ref, acc_ref, *, topk, tq):
    i = pl.program_id(1)
    nk = i + 1
    w = wit_ref[...]

    def chunk(c):
        return pl.ds(pl.multiple_of(c * tq, tq), tq)

    def score_chunk(c, diagonal):
        kc = ki4_ref[chunk(c), :]
        sc = None
        for h in range(N_IDX_HEADS):
            s = lax.dot_general(kc, qi4_ref[:, 4 * IDX_DIM * h:4 * IDX_DIM * (h + 1)], NT_DIMS,
                                preferred_element_type=F32)
            t = jnp.maximum(s, 0.0) * w[h:h + 1, :]
            sc = t if sc is None else sc + t
        key = _sortable(sc)
        if diagonal:
            kpos = lax.broadcasted_iota(I32, key.shape, 0)
            qpos = lax.broadcasted_iota(I32, key.shape, 1)
            key = jnp.where(kpos <= qpos, key, INT_MIN)
        keys_ref[chunk(c), :] = key
        khi_ref[chunk(c), :] = (key >> 16).astype(I16)

    def score_pair(cc, _):
        score_chunk(2 * cc, False)
        score_chunk(2 * cc + 1, False)
        return 0

    lax.fori_loop(0, i // 2, score_pair, 0)

    @pl.when(i % 2 == 1)
    def _():
        score_chunk(i - 1, False)

    score_chunk(i, True)

    pack_rows = 2 * SUBLANES
    n_acc = 4

    def count16(ref, cand):
        cand = cand.astype(I16)

        def body(c, accs):
            m = jnp.where(ref[chunk(c), :] >= cand, jnp.ones((), I16), jnp.zeros((), I16))
            m = m.reshape(tq // pack_rows, pack_rows, tq)
            accs = list(accs)
            for r in range(tq // pack_rows):
                accs[r % n_acc] = accs[r % n_acc] + m[r]
            return tuple(accs)

        accs = lax.fori_loop(0, nk, body, tuple(jnp.zeros((pack_rows, tq), I16) for _ in range(n_acc)))
        tot = (accs[0] + accs[1]) + (accs[2] + accs[3])
        return jnp.sum(tot.astype(I32), axis=0, keepdims=True)

    def search16(ref):
        def bit_body(b, t):
            cand = t + (jnp.int32(1) << (15 - b))
            return jnp.where(count16(ref, cand) >= topk, cand, t)
        return lax.fori_loop(0, 16, bit_body, jnp.full((1, tq), I16_MIN, I32))

    tau_hi = search16(khi_ref)

    def lo_body(c, _):
        k = keys_ref[chunk(c), :]
        hi = k >> 16
        lo = (k & 0xFFFF) + I16_MIN
        lo = jnp.where(hi == tau_hi, lo, jnp.where(hi > tau_hi, I16_MAX, I16_MIN))
        klo_ref[chunk(c), :] = lo.astype(I16)
        return 0

    lax.fori_loop(0, nk, lo_body, 0)
    tau_lo = search16(klo_ref)
    tau = (tau_hi << 16) + (tau_lo - I16_MIN)
    tau = jnp.maximum(tau, INT_MIN + 1)

    def gt_body(c, acc):
        m = jnp.where(keys_ref[chunk(c), :] > tau, 1, 0).astype(I32)
        return acc + jnp.sum(m.reshape(tq // SUBLANES, SUBLANES, tq), axis=0)

    n_gt = jnp.sum(lax.fori_loop(0, nk, gt_body, jnp.zeros((SUBLANES, tq), I32)), axis=0, keepdims=True)
    quota = (topk - n_gt).astype(F32)

    for h in range(N_HEADS):
        qs_ref[h * tq:(h + 1) * tq, :] = q_ref[:, h * LANES:(h + 1) * LANES]
    m_ref[...] = jnp.full(m_ref.shape, NEG, F32)
    acc_ref[...] = jnp.zeros(acc_ref.shape, F32)
    r_i = lax.broadcasted_iota(I32, (tq, tq), 0)
    c_i = lax.broadcasted_iota(I32, (tq, tq), 1)
    ltri = jnp.where(c_i <= r_i, 1.0, 0.0).astype(BF16)

    def prepare(c, slot):
        cc = jnp.minimum(c, nk - 1)
        kc = keys_ref[chunk(cc), :]
        eq = kc == tau
        e = jnp.where(eq, 1.0, 0.0)
        pre = jnp.dot(ltri, e.astype(BF16), preferred_element_type=F32)
        rank = tie_ref[...] + pre - e
        sel = ((kc > tau) | (eq & (rank < quota))) & (c < nk)
        tie_ref[...] = tie_ref[...] + pre[tq - 1:tq, :]
        bias = jnp.where(sel, 0.0, NEG)
        lg = lax.dot_general(kb_ref[chunk(cc), :], qs_ref[...], NT_DIMS, preferred_element_type=F32)
        for h in range(N_HEADS):
            lg_ref[slot, :, h * tq:(h + 1) * tq] = lg[:, h * tq:(h + 1) * tq] + bias

    def softmax(slot):
        for j in range(N_HEADS * tq // LANES):
            cols = slice(j * LANES, (j + 1) * LANES)
            x = lg_ref[slot, :, cols]
            m_prev = m_ref[:, cols]
            m_new = jnp.maximum(m_prev, jnp.max(x, axis=0, keepdims=True))
            alpha = jnp.exp2(m_prev - m_new)
            m_ref[:, cols] = m_new
            p_ref[slot, :, cols] = jnp.exp2(x - m_new).astype(BF16)
            alpha_ref[slot, :, cols] = alpha

    def value(c, slot):
        cc = jnp.clip(c, 0, nk - 1)
        acc_ref[...] = alpha_ref[slot] * acc_ref[...] + jnp.dot(vt_ref[:, chunk(cc)], p_ref[slot],
                                                                preferred_element_type=F32)

    tie_ref[...] = jnp.zeros(tie_ref.shape, F32)
    p_ref[1] = jnp.zeros(p_ref.shape[1:], BF16)
    alpha_ref[1] = jnp.ones(alpha_ref.shape[1:], F32)
    prepare(0, 0)

    def pair_body(cc, _):
        for slot in range(2):
            c = 2 * cc + slot
            prepare(c + 1, 1 - slot)
            softmax(slot)
            value(c - 1, 1 - slot)
        return 0

    n_pairs = (nk + 1) // 2
    lax.fori_loop(0, n_pairs, pair_body, 0)
    value(2 * n_pairs - 1, 1)
    out = acc_ref[:KV_WIDTH, :] / acc_ref[KV_WIDTH:KV_WIDTH + 1, :]
    for h in range(N_HEADS):
        o_ref[:, h * LANES:(h + 1) * LANES] = out[:, h * tq:(h + 1) * tq].T.astype(BF16)


def _prompt_attention(qi4, wit, ki4, q, kb, vt, tq):
    B, T, _ = q.shape
    topk = min(TOPK_MAX, T // 4)

    def tile(w):
        return pl.BlockSpec((None, tq, w), lambda b, i: (b, i, 0))

    def full(w):
        return pl.BlockSpec((None, T, w), lambda b, i: (b, 0, 0))

    return pl.pallas_call(
        functools.partial(_attn_kernel, topk=topk, tq=tq),
        grid=(B, T // tq),
        in_specs=[tile(4 * IDX_DIM * N_IDX_HEADS),
                  pl.BlockSpec((None, N_IDX_HEADS, tq), lambda b, i: (b, 0, i)),
                  full(4 * IDX_DIM), tile(QPAD), full(KV_WIDTH),
                  pl.BlockSpec((None, VT_ROWS, T), lambda b, i: (b, 0, 0))],
        out_specs=tile(QPAD),
        out_shape=jax.ShapeDtypeStruct((B, T, QPAD), BF16),
        scratch_shapes=[pltpu.VMEM((T, tq), I32), pltpu.VMEM((T, tq), I16), pltpu.VMEM((T, tq), I16),
                        pltpu.VMEM((N_HEADS * tq, LANES), BF16), pltpu.VMEM((2, tq, N_HEADS * tq), F32),
                        pltpu.VMEM((2, tq, N_HEADS * tq), BF16),
                        pltpu.VMEM((2, 1, N_HEADS * tq), F32), pltpu.VMEM((1, tq), F32),
                        pltpu.VMEM((1, N_HEADS * tq), F32), pltpu.VMEM((VT_ROWS, N_HEADS * tq), F32)],
        compiler_params=_cparams(2),
        name="prompt_attention",
    )(qi4, wit, ki4, q, kb, vt)


def _s5_prep_kernel(lr_ref, li_ref, ldt_ref, brt_ref, bit_ref, ar_ref, ai_ref, bbr_ref, bbi_ref):
    lr = lr_ref[...]
    li = li_ref[...]
    dt = jnp.exp(ldt_ref[...])
    mag = jnp.exp(lr * dt)
    ar = mag * jnp.cos(li * dt)
    ai = mag * jnp.sin(li * dt)
    den = lr * lr + li * li
    cr = ((ar - 1.0) * lr + ai * li) / den
    ci = (ai * lr - (ar - 1.0) * li) / den
    ar_ref[...] = ar
    ai_ref[...] = ai
    brt = brt_ref[...]
    bit = bit_ref[...]
    bbr_ref[...] = cr[:, None, :] * brt - ci[:, None, :] * bit
    bbi_ref[...] = cr[:, None, :] * bit + ci[:, None, :] * brt


def _s5_weights(lambda_re, lambda_im, log_dt, b_re, b_im, c_re, c_im):
    G, P, C = N_SSM_GROUPS, SSM_STATE, SSM_GROUP
    ar, ai, bbr, bbi = pl.pallas_call(
        _s5_prep_kernel,
        out_shape=(jax.ShapeDtypeStruct((G, P), F32), jax.ShapeDtypeStruct((G, P), F32),
                   jax.ShapeDtypeStruct((G, C, P), F32), jax.ShapeDtypeStruct((G, C, P), F32)),
        name="s5_prep",
    )(lambda_re, lambda_im, log_dt.reshape(G, 1), jnp.swapaxes(b_re, 1, 2), jnp.swapaxes(b_im, 1, 2))
    eye = jnp.eye(GROUPS_PER_CHUNK, dtype=F32)

    def in_blocks(bb):
        x = bb.reshape(N_UCHUNK, GROUPS_PER_CHUNK, C, P)
        x = x[:, :, :, None, :] * eye[None, :, None, :, None]
        return x.reshape(N_UCHUNK, LANES, GROUPS_PER_CHUNK * P)

    def out_blocks(cc):
        x = jnp.swapaxes(cc, 1, 2).reshape(N_UCHUNK, GROUPS_PER_CHUNK, P, C)
        x = x[:, :, :, None, :] * eye[None, :, None, :, None]
        return x.reshape(N_UCHUNK, GROUPS_PER_CHUNK * P, LANES)

    bblk = jnp.concatenate([in_blocks(bbr), in_blocks(bbi)], axis=2).astype(BF16)
    cblk = jnp.concatenate([out_blocks(c_re), out_blocks(-c_im)], axis=1).astype(BF16)
    half = SLABS_PER_CHUNK // 2

    def slabs(a):
        x = a.reshape(N_UCHUNK, 1, half, 1, LANES)
        return jnp.broadcast_to(x, (N_UCHUNK, 2, half, 1, LANES)).reshape(N_SLAB, 1, LANES)

    return slabs(ar), slabs(ai), bblk, cblk


def _state_to_slabs(h_re, h_im):
    B = h_re.shape[0]
    half = SLABS_PER_CHUNK // 2
    re = h_re.reshape(B, N_UCHUNK, 1, half, LANES)
    im = h_im.reshape(B, N_UCHUNK, 1, half, LANES)
    x = jnp.concatenate([re, im], axis=2).reshape(B, N_SLAB, LANES)
    return jnp.swapaxes(x, 0, 1)


def _slabs_to_state(h):
    B = h.shape[1]
    half = SLABS_PER_CHUNK // 2
    x = jnp.swapaxes(h, 0, 1).reshape(B, N_UCHUNK, 2, half * LANES)
    re = x[:, :, 0].reshape(B, N_SSM_GROUPS, SSM_STATE)
    im = x[:, :, 1].reshape(B, N_SSM_GROUPS, SSM_STATE)
    return re, im


def _s5_kernel(u_ref, h0_ref, ar_ref, ai_ref, bblk_ref, cblk_ref, d_ref, y_ref, hout_ref, bu_ref, hs_ref,
               *, nb, tj, lane_batches):
    rows = tj * nb
    half = SLABS_PER_CHUNK // 2
    blk_rows = rows // lane_batches

    @pl.when(pl.program_id(0) == 0)
    def _():
        hout_ref[...] = h0_ref[...]

    def batch_rows(lb):
        if lane_batches == 1:
            return pl.ds(0, blk_rows)
        return pl.ds(lb, blk_rows, stride=lane_batches)

    def u_cols(lb, q):
        return slice(lb * SSM_WIDTH + q * LANES, lb * SSM_WIDTH + (q + 1) * LANES)

    for q in range(N_UCHUNK):
        ub = jnp.concatenate([u_ref[:, u_cols(lb, q)] for lb in range(lane_batches)], axis=0).astype(BF16)
        bu = jnp.dot(ub, bblk_ref[q], preferred_element_type=F32)
        for lb in range(lane_batches):
            for s in range(SLABS_PER_CHUNK):
                bu_ref[q * SLABS_PER_CHUNK + s, batch_rows(lb), :] = bu[lb * blk_rows:(lb + 1) * blk_rows,
                                                                        s * LANES:(s + 1) * LANES]

    def step(j, h):
        cur = pl.ds(j * nb, nb)
        new = list(h)
        for q in range(N_UCHUNK):
            for r in range(half):
                sr = q * SLABS_PER_CHUNK + r
                si = sr + half
                a_re = ar_ref[sr]
                a_im = ai_ref[sr]
                new[sr] = a_re * h[sr] - a_im * h[si] + bu_ref[sr, cur, :]
                new[si] = a_re * h[si] + a_im * h[sr] + bu_ref[si, cur, :]
                hs_ref[sr, cur, :] = new[sr]
                hs_ref[si, cur, :] = new[si]
        return tuple(new)

    h = lax.fori_loop(0, tj, step, tuple(hout_ref[s] for s in range(N_SLAB)))
    for s in range(N_SLAB):
        hout_ref[s] = h[s]

    for q in range(N_UCHUNK):
        hq = jnp.concatenate(
            [jnp.concatenate([hs_ref[q * SLABS_PER_CHUNK + s, batch_rows(lb), :] for s in range(SLABS_PER_CHUNK)],
                             axis=1) for lb in range(lane_batches)], axis=0).astype(BF16)
        yq = jnp.dot(hq, cblk_ref[q], preferred_element_type=F32)
        for lb in range(lane_batches):
            cols = u_cols(lb, q)
            y_ref[:, cols] = (yq[lb * blk_rows:(lb + 1) * blk_rows]
                              + d_ref[:, q * LANES:(q + 1) * LANES] * u_ref[:, cols])


def _s5_scan(u, nb, lane_batches, h0_slabs, weights, d_skip, tj):
    T = u.shape[0] * lane_batches // nb
    ar, ai, bblk, cblk = weights
    rows = tj * nb
    blk = (rows // lane_batches, lane_batches * SSM_WIDTH)
    y, hout = pl.pallas_call(
        functools.partial(_s5_kernel, nb=nb, tj=tj, lane_batches=lane_batches),
        grid=(T // tj,),
        in_specs=[pl.BlockSpec(blk, lambda t: (t, 0)),
                  _const_spec((N_SLAB, nb, LANES)), _const_spec((N_SLAB, 1, LANES)),
                  _const_spec((N_SLAB, 1, LANES)),
                  _const_spec((N_UCHUNK, LANES, SLABS_PER_CHUNK * LANES)),
                  _const_spec((N_UCHUNK, SLABS_PER_CHUNK * LANES, LANES)),
                  _const_spec((1, SSM_WIDTH))],
        out_specs=(pl.BlockSpec(blk, lambda t: (t, 0)),
                   pl.BlockSpec((N_SLAB, nb, LANES), lambda t: (0, 0, 0))),
        out_shape=(jax.ShapeDtypeStruct(u.shape, F32),
                   jax.ShapeDtypeStruct((N_SLAB, nb, LANES), F32)),
        scratch_shapes=[pltpu.VMEM((N_SLAB, rows, LANES), F32), pltpu.VMEM((N_SLAB, rows, LANES), F32)],
        compiler_params=_cparams(1),
        name="s5_scan",
    )(u, h0_slabs, ar, ai, bblk, cblk, d_skip.reshape(1, SSM_WIDTH))
    return y, hout


def _post_kernel(x_ref, ao_ref, y_ref, wglu_ref, bglu_ref, woa_ref, wos_ref, gpost_ref, gpre_ref,
                 wup_ref, wdn_ref, gfpost_ref, o_ref):
    z = jax.nn.gelu(y_ref[...])
    gate = jax.nn.sigmoid(jnp.dot(z.astype(BF16), wglu_ref[...], preferred_element_type=F32) + bglu_ref[...])
    so = z * gate
    mix = (jnp.dot(ao_ref[...], woa_ref[...], preferred_element_type=F32)
           + jnp.dot(so.astype(BF16), wos_ref[...], preferred_element_type=F32))
    x = x_ref[...] + _rms(mix, gpost_ref[...])
    hdn = jnp.dot(_rms(x, gpre_ref[...]).astype(BF16), wup_ref[...], preferred_element_type=F32)
    hdn = jnp.square(jnp.maximum(hdn, 0.0))
    dn = jnp.dot(hdn.astype(BF16), wdn_ref[...], preferred_element_type=F32)
    o_ref[...] = x + _rms(dn, gfpost_ref[...])


def _post_weights(w_glu, w_out, w_up, w_down):
    zero = jnp.zeros((HEAD_DIM, D_MODEL), F32)
    rows = []
    for h in range(N_HEADS):
        wh = w_out[h * HEAD_DIM:(h + 1) * HEAD_DIM]
        rows += [wh, zero] if h // Q_PER_KV == 0 else [zero, wh]
    woa = jnp.concatenate(rows, axis=0).astype(BF16)
    return (w_glu.astype(BF16), woa, w_out[ATTN_WIDTH:].astype(BF16), w_up.astype(BF16), w_down.astype(BF16))


def _post(x, ao, y_tb, weights, b_glu, g_post, g_ffn_pre, g_ffn_post, tm):
    B, T, _ = x.shape
    wglu, woa, wos, wup, wdn = weights

    def row(w):
        return pl.BlockSpec((None, tm, w), lambda b, t: (b, t, 0))

    def vec(a):
        return a.reshape(1, -1)

    return pl.pallas_call(
        _post_kernel,
        grid=(B, T // tm),
        in_specs=[row(D_MODEL), row(QPAD), pl.BlockSpec((tm, SSM_WIDTH), lambda b, t: (t, b)),
                  _const_spec((SSM_WIDTH, SSM_WIDTH)), _const_spec((1, SSM_WIDTH)),
                  _const_spec((QPAD, D_MODEL)), _const_spec((SSM_WIDTH, D_MODEL)),
                  _const_spec((1, D_MODEL)), _const_spec((1, D_MODEL)),
                  _const_spec((D_MODEL, D_FF)), _const_spec((D_FF, D_MODEL)), _const_spec((1, D_MODEL))],
        out_specs=row(D_MODEL),
        out_shape=jax.ShapeDtypeStruct((B, T, D_MODEL), F32),
        compiler_params=_cparams(2),
        name="post",
    )(x, ao, y_tb, wglu, vec(b_glu), woa, wos, vec(g_post), vec(g_ffn_pre), wup, wdn, vec(g_ffn_post))


def _page_map(b, j, pt, *, r, n):
    return (pt[b, j * n + r], 0, 0)


def _skeys_kernel(pt_ref, qh_ref, ql_ref, w_ref, kn_ref, *rest, n_pages_step, dec_t):
    page_refs = rest[:n_pages_step]
    keys_ref, knew_ref = rest[n_pages_step:]
    j = pl.program_id(1)
    qh = qh_ref[...]
    ql = ql_ref[...]
    w = w_ref[...]

    def scores(kc):
        kh, kl = _split_bf16(kc)
        s = (jnp.dot(qh, kh, preferred_element_type=F32) + jnp.dot(ql, kh, preferred_element_type=F32)
             + jnp.dot(qh, kl, preferred_element_type=F32))
        sc = None
        for h in range(N_IDX_HEADS):
            t = jnp.maximum(s[h * dec_t:(h + 1) * dec_t], 0.0) * w[:, h:h + 1]
            sc = t if sc is None else sc + t
        return _sortable(sc)

    keys_ref[...] = scores(jnp.concatenate([r[...] for r in page_refs], axis=1))

    @pl.when(j == pl.num_programs(1) - 1)
    def _():
        kn = scores(kn_ref[...])
        qi_ = lax.broadcasted_iota(I32, kn.shape, 0)
        ki_ = lax.broadcasted_iota(I32, kn.shape, 1)
        knew_ref[...] = jnp.where(ki_ <= qi_, kn, INT_MIN)


def _ssearch_kernel(kp_ref, kn_ref, sel_ref, *, topk, past_len):
    def count(pred_past, pred_new):
        c = jnp.sum(jnp.where(pred_past, 1, 0).astype(I32), axis=1, keepdims=True)
        return c + jnp.sum(jnp.where(pred_new, 1, 0).astype(I32), axis=1, keepdims=True)

    def count_ge(cand):
        return count(kp_ref[...] >= cand, kn_ref[...] >= cand)

    rows = kp_ref.shape[0]
    tau = jnp.where(count_ge(jnp.zeros((rows, 1), I32)) >= topk, 0, INT_MIN).astype(I32)

    def bit_body(b, tau):
        cand = tau + (jnp.int32(1) << (30 - b))
        return jnp.where(count_ge(cand) >= topk, cand, tau)

    tau = lax.fori_loop(0, 31, bit_body, tau)
    tau = jnp.maximum(tau, INT_MIN + 1)
    quota = topk - count(kp_ref[...] > tau, kn_ref[...] > tau)
    pos_p = lax.broadcasted_iota(I32, kp_ref.shape, 1)
    pos_n = past_len + lax.broadcasted_iota(I32, kn_ref.shape, 1)
    n_bits = max(1, (past_len + kn_ref.shape[1] - 1).bit_length())

    def idx_body(b, lim):
        cand = lim + (jnp.int32(1) << (n_bits - 1 - b))
        ties = count((kp_ref[...] == tau) & (pos_p < cand), (kn_ref[...] == tau) & (pos_n < cand))
        return jnp.where(ties < quota, cand, lim)

    lim = lax.fori_loop(0, n_bits, idx_body, jnp.zeros_like(tau))
    kp = kp_ref[...]
    kn = kn_ref[...]
    sel_ref[:, :past_len] = jnp.where((kp > tau) | ((kp == tau) & (pos_p <= lim)), 1.0, 0.0)
    sel_ref[:, past_len:] = jnp.where((kn > tau) | ((kn == tau) & (pos_n <= lim)), 1.0, 0.0)


def _sample_select(page_table, qh, ql, w, ki_new_pad, cache_kidx, n_pages_step):
    DB, n_pages = page_table.shape
    dec_t = w.shape[1]
    past_len = n_pages * PAGE_SIZE
    lp = past_len + LANES
    topk = min(TOPK_MAX, (past_len + dec_t) // 4)
    rows = N_IDX_HEADS * dec_t
    step_keys = n_pages_step * PAGE_SIZE

    def fixed(shape):
        return pl.BlockSpec((None,) + shape, lambda b, j, pt: (b, 0, 0))

    pages = [pl.BlockSpec((None, IDX_DIM, PAGE_SIZE), functools.partial(_page_map, r=r, n=n_pages_step))
             for r in range(n_pages_step)]
    grid_spec = pltpu.PrefetchScalarGridSpec(
        num_scalar_prefetch=1,
        grid=(DB, n_pages // n_pages_step),
        in_specs=[fixed((rows, IDX_DIM)), fixed((rows, IDX_DIM)), fixed((dec_t, N_IDX_HEADS)),
                  fixed((IDX_DIM, LANES))] + pages,
        out_specs=(pl.BlockSpec((None, dec_t, step_keys), lambda b, j, pt: (b, 0, j)), fixed((dec_t, LANES))),
    )
    keys_past, keys_new = pl.pallas_call(
        functools.partial(_skeys_kernel, n_pages_step=n_pages_step, dec_t=dec_t),
        grid_spec=grid_spec,
        out_shape=(jax.ShapeDtypeStruct((DB, dec_t, past_len), I32), jax.ShapeDtypeStruct((DB, dec_t, LANES), I32)),
        compiler_params=_cparams(2),
        name="sample_keys",
    )(page_table, qh, ql, w, ki_new_pad, *([cache_kidx] * n_pages_step))
    sel = pl.pallas_call(
        functools.partial(_ssearch_kernel, topk=topk, past_len=past_len),
        out_shape=jax.ShapeDtypeStruct((DB * dec_t, lp), F32),
        compiler_params=pltpu.CompilerParams(vmem_limit_bytes=VMEM_LIMIT),
        name="sample_search",
    )(keys_past.reshape(DB * dec_t, past_len), keys_new.reshape(DB * dec_t, LANES))
    return sel.reshape(DB, dec_t, lp)


def _sattn_kernel(pt_ref, sel_ref, q_ref, kn_ref, vn_ref, *rest, n_pages_step, past_len, dec_t):
    k_refs = rest[:n_pages_step]
    v_refs = rest[n_pages_step:2 * n_pages_step]
    o_ref, m_ref, l_ref, acc_ref = rest[2 * n_pages_step:]
    j = pl.program_id(1)
    step_keys = n_pages_step * PAGE_SIZE
    q = q_ref[...]

    @pl.when(j == 0)
    def _():
        m_ref[...] = jnp.full(m_ref.shape, NEG, F32)
        l_ref[...] = jnp.zeros(l_ref.shape, F32)
        acc_ref[...] = jnp.zeros(acc_ref.shape, F32)

    def update(kc, vc, sel):
        n = kc.shape[1]
        lg = jnp.dot(q, kc, preferred_element_type=F32)
        lg = jnp.where(sel[None] > 0.5, lg.reshape(N_HEADS, dec_t, n), NEG).reshape(N_HEADS * dec_t, n)
        m_prev = m_ref[...]
        m_new = jnp.maximum(m_prev, jnp.max(lg, axis=1, keepdims=True))
        alpha = jnp.exp2(m_prev - m_new)
        p = jnp.exp2(lg - m_new)
        l_ref[...] = alpha * l_ref[...] + jnp.sum(p, axis=1, keepdims=True)
        acc_ref[...] = alpha * acc_ref[...] + lax.dot_general(p.astype(BF16), vc, NT_DIMS,
                                                              preferred_element_type=F32)
        m_ref[...] = m_new

    kc = jnp.concatenate([r[...] for r in k_refs], axis=1).astype(BF16)
    vc = jnp.concatenate([r[...] for r in v_refs], axis=1).astype(BF16)
    update(kc, vc, sel_ref[:, pl.ds(pl.multiple_of(j * step_keys, step_keys), step_keys)])

    @pl.when(j == pl.num_programs(1) - 1)
    def _():
        update(kn_ref[...], vn_ref[...], sel_ref[:, past_len:])
        o_ref[...] = acc_ref[...] / l_ref[...]


def _sample_attention(page_table, sel, q, k_new_pad, v_new_pad, cache_k, cache_v, n_pages_step):
    DB, n_pages = page_table.shape
    dec_t = sel.shape[1]
    past_len = n_pages * PAGE_SIZE
    lp = past_len + LANES
    rows = N_HEADS * dec_t

    def fixed(shape):
        return pl.BlockSpec((None,) + shape, lambda b, j, pt: (b, 0, 0))

    def pages():
        return [pl.BlockSpec((None, KV_WIDTH, PAGE_SIZE), functools.partial(_page_map, r=r, n=n_pages_step))
                for r in range(n_pages_step)]

    grid_spec = pltpu.PrefetchScalarGridSpec(
        num_scalar_prefetch=1,
        grid=(DB, n_pages // n_pages_step),
        in_specs=[fixed((dec_t, lp)), fixed((rows, KV_WIDTH)), fixed((KV_WIDTH, LANES)),
                  fixed((KV_WIDTH, LANES))] + pages() + pages(),
        out_specs=fixed((rows, KV_WIDTH)),
        scratch_shapes=[pltpu.VMEM((rows, 1), F32), pltpu.VMEM((rows, 1), F32),
                        pltpu.VMEM((rows, KV_WIDTH), F32)],
    )
    return pl.pallas_call(
        functools.partial(_sattn_kernel, n_pages_step=n_pages_step, past_len=past_len, dec_t=dec_t),
        grid_spec=grid_spec,
        out_shape=jax.ShapeDtypeStruct((DB, rows, KV_WIDTH), F32),
        compiler_params=_cparams(2),
        name="sample_attention",
    )(page_table, sel, q, k_new_pad, v_new_pad, *([cache_k] * n_pages_step), *([cache_v] * n_pages_step))


def _largest_tile(n, cap):
    t = min(n, cap)
    while n % t:
        t //= 2
    return t


def _layer(hp, hs, cache_k, cache_v, cache_kidx, st_re, st_im, page_table, g_pre, w_in, ssm_params, d_skip,
           w_glu, b_glu, w_out, g_post, g_ffn_pre, w_up, w_down, g_ffn_post):
    B, T, _ = hp.shape
    DB, dec_t, _ = hs.shape
    n_pages = page_table.shape[1]
    past_len = n_pages * PAGE_SIZE
    idx_scale = (IDX_DIM ** -0.5) * (N_IDX_HEADS ** -0.5)

    wm, wih, wil = _inproj_weights(w_in)
    s5w = _s5_weights(*ssm_params)
    postw = _post_weights(w_glu, w_out, w_up, w_down)

    pos_p = jnp.arange(T, dtype=I32)
    q, kft, kb, vft, vt, u_tb, kift, ki4, qi4, misc = _inproj(hp, pos_p, g_pre, wm, wih, wil, _largest_tile(T, 512))
    wit = jnp.swapaxes(misc[:, :, :N_IDX_HEADS], 1, 2) * idx_scale
    ao = _prompt_attention(qi4, wit, ki4, q, kb, vt, _largest_tile(T, 256))
    h0 = jnp.zeros((N_SLAB, B, LANES), F32)
    y_tb, hfin = _s5_scan(u_tb, B, B, h0, s5w, d_skip, _largest_tile(T, 256))
    yp = _post(hp, ao, y_tb, postw, b_glu, g_post, g_ffn_pre, g_ffn_post, _largest_tile(T, 512))
    srp, sip = _slabs_to_state(hfin)

    def kv_rows(a, nb, t):
        return jnp.transpose(a.reshape(nb, N_KV_HEADS, HEAD_DIM, t), (0, 3, 1, 2))

    prompt_out = (yp, kv_rows(kft, B, T), kv_rows(vft, B, T), jnp.swapaxes(kift, 1, 2), srp, sip)

    n_s = DB * dec_t
    pos_s = jnp.tile(past_len + jnp.arange(dec_t, dtype=I32), DB)
    q, kft, _, vft, _, u_s, kift, _, qi4, misc = _inproj(hs.reshape(1, n_s, D_MODEL), pos_s, g_pre, wm, wih, wil,
                                                        _largest_tile(n_s, 256))
    qi4 = qi4.reshape(DB, dec_t, N_IDX_HEADS, 4 * IDX_DIM)

    def heads_first(a):
        return jnp.swapaxes(a, 1, 2).reshape(DB, a.shape[2] * dec_t, a.shape[3])

    def per_batch(a):
        return jnp.swapaxes(a.reshape(a.shape[1], DB, dec_t), 0, 1)

    def pad_new(a, dtype):
        return jnp.pad(a, ((0, 0), (0, 0), (0, LANES - dec_t))).astype(dtype)

    k_s, v_s, ki_s = per_batch(kft), per_batch(vft), per_batch(kift)
    qh = heads_first(qi4[..., :IDX_DIM])
    ql = heads_first(qi4[..., 2 * IDX_DIM:3 * IDX_DIM])
    w_s = misc.reshape(DB, dec_t, LANES)[:, :, :N_IDX_HEADS] * idx_scale

    n_phys = cache_k.shape[0]
    pool_ki = jnp.swapaxes(cache_kidx, 1, 2)
    pool_k = jnp.transpose(cache_k, (0, 2, 3, 1)).reshape(n_phys, KV_WIDTH, PAGE_SIZE)
    pool_v = jnp.transpose(cache_v, (0, 2, 3, 1)).reshape(n_phys, KV_WIDTH, PAGE_SIZE)
    n_step = _largest_tile(n_pages, 32)
    sel = _sample_select(page_table, qh, ql, w_s, pad_new(ki_s, F32), pool_ki, n_step)
    qs = heads_first(q.reshape(DB, dec_t, N_HEADS, LANES))
    ao_s = _sample_attention(page_table, sel, qs, pad_new(k_s, BF16), pad_new(v_s, BF16), pool_k, pool_v, n_step)
    ao_s = jnp.swapaxes(ao_s.reshape(DB, N_HEADS, dec_t, LANES), 1, 2).reshape(1, n_s, QPAD).astype(BF16)
    u_t = jnp.swapaxes(u_s.reshape(DB, dec_t, SSM_WIDTH), 0, 1).reshape(n_s, SSM_WIDTH)
    y_t, hfin = _s5_scan(u_t, DB, 1, _state_to_slabs(st_re, st_im), s5w, d_skip, dec_t)
    y_s = jnp.swapaxes(y_t.reshape(dec_t, DB, SSM_WIDTH), 0, 1).reshape(n_s, SSM_WIDTH)
    ys = _post(hs.reshape(1, n_s, D_MODEL), ao_s, y_s, postw, b_glu, g_post, g_ffn_pre, g_ffn_post,
               _largest_tile(n_s, 256))
    srs, sis = _slabs_to_state(hfin)
    sample_out = (ys.reshape(DB, dec_t, D_MODEL), kv_rows(k_s, DB, dec_t), kv_rows(v_s, DB, dec_t),
                  jnp.swapaxes(ki_s, 1, 2), srs, sis)
    return prompt_out, sample_out


def kernel(x_prompt, x_sample, cache_k, cache_v, cache_kidx, state_ssm_re, state_ssm_im, page_table, norm_mix_pre, w_in, lambda_re, lambda_im, log_dt, b_re, b_im, c_re, c_im, d_skip, w_glu, b_glu, w_out, norm_mix_post, norm_ffn_pre, w_up, w_down, norm_ffn_post):
    depth = w_in.shape[0]
    hp, hs = x_prompt, x_sample
    p_acc = [[] for _ in range(5)]
    s_acc = [[] for _ in range(5)]
    for l in range(depth):
        ssm_params = (lambda_re[l], lambda_im[l], log_dt[l], b_re[l], b_im[l], c_re[l], c_im[l])
        p_out, s_out = _layer(hp, hs, cache_k[l], cache_v[l], cache_kidx[l], state_ssm_re[l], state_ssm_im[l],
                              page_table, norm_mix_pre[l], w_in[l], ssm_params, d_skip[l], w_glu[l], b_glu[l],
                              w_out[l], norm_mix_post[l], norm_ffn_pre[l], w_up[l], w_down[l], norm_ffn_post[l])
        hp, hs = p_out[0], s_out[0]
        for acc, vals in ((p_acc, p_out[1:]), (s_acc, s_out[1:])):
            for a, v in zip(acc, vals):
                a.append(v)
    return (hp, hs) + tuple(jnp.stack(a) for a in p_acc) + tuple(jnp.stack(a) for a in s_acc)
```

```python
import functools
import math

import jax
import jax.numpy as jnp
from jax import lax
from jax.experimental import pallas as pl
from jax.experimental.pallas import tpu as pltpu

F32 = jnp.float32
BF16 = jnp.bfloat16
I32 = jnp.int32

D_MODEL = 1024
PAGE_SIZE = 128
HEAD_DIM = 64
ATTN_WIDTH = D_MODEL // 2
N_HEADS = ATTN_WIDTH // HEAD_DIM
N_KV_HEADS = 2
Q_PER_KV = N_HEADS // N_KV_HEADS
KV_WIDTH = N_KV_HEADS * HEAD_DIM
N_IDX_HEADS = 4
IDX_DIM = 64
TOPK_MAX = 256
ROPE_THETA = 10000.0
SSM_WIDTH = D_MODEL - ATTN_WIDTH
SSM_GROUP = 16
N_SSM_GROUPS = SSM_WIDTH // SSM_GROUP
SSM_STATE = 64
D_FF = 4 * D_MODEL
RMS_EPS = 1e-6
NEG = -1e30
OFF_Q = 0
OFF_K = OFF_Q + ATTN_WIDTH
OFF_V = OFF_K + KV_WIDTH
OFF_QI = OFF_V + KV_WIDTH
OFF_KI = OFF_QI + N_IDX_HEADS * IDX_DIM
OFF_WI = OFF_KI + IDX_DIM
OFF_U = OFF_WI + N_IDX_HEADS
IN_WIDTH = OFF_U + SSM_WIDTH

LANES = 128
SUBLANES = 8
INT_MIN = -(2 ** 31)
VMEM_LIMIT = 56 * 1024 * 1024

QPAD = N_HEADS * LANES
MAIN_W = QPAD + 2 * KV_WIDTH + SSM_WIDTH
IDX_W = 2 * N_IDX_HEADS * IDX_DIM + 2 * IDX_DIM + LANES
SSM_LANES = N_SSM_GROUPS * SSM_STATE
N_SLAB = 2 * SSM_LANES // LANES
N_UCHUNK = SSM_WIDTH // LANES
GROUPS_PER_CHUNK = LANES // SSM_GROUP
SLABS_PER_CHUNK = N_SLAB // N_UCHUNK

NT_DIMS = (((1,), (1,)), ((), ()))
LOG2_E = math.log2(math.e)
I16 = jnp.int16
I16_MIN = -(2 ** 15)
I16_MAX = 2 ** 15 - 1
VT_ROWS = KV_WIDTH + 2 * SUBLANES
EXP2_FLOOR = -256.0


def _cparams(n_grid):
    return pltpu.CompilerParams(dimension_semantics=("arbitrary",) * n_grid,
                                vmem_limit_bytes=VMEM_LIMIT)


def _const_spec(shape):
    zeros = (0,) * len(shape)
    return pl.BlockSpec(shape, lambda *_: zeros, pipeline_mode=pl.Buffered(1))


def _rms(x, g):
    return x * lax.rsqrt(jnp.mean(x * x, axis=-1, keepdims=True) + RMS_EPS) * g


def _split_bf16(x):
    hi = x.astype(BF16)
    lo = (x - hi.astype(F32)).astype(BF16)
    return hi, lo


def _sortable(x):
    b = lax.bitcast_convert_type(x, I32)
    return b ^ ((b >> 31) & jnp.int32(0x7FFFFFFF))


def _rope(x, cos, sin):
    w = x.shape[-1]
    lane = lax.broadcasted_iota(I32, x.shape, 1)
    first = (lane & (HEAD_DIM - 1)) < HEAD_DIM // 2
    sw = jnp.where(first, pltpu.roll(x, w - HEAD_DIM // 2, 1), pltpu.roll(x, HEAD_DIM // 2, 1))
    return x * cos + sw * sin


def _inproj_kernel(x_ref, g_ref, wm_ref, wih_ref, wil_ref, cos_ref, sin_ref,
                   q_ref, kf_ref, kb_ref, vf_ref, vt_ref, u_ref, kif_ref, ki4_ref, qi4_ref, misc_ref):
    xn = _rms(x_ref[...], g_ref[...])
    xh, xl = _split_bf16(xn)
    cos = cos_ref[...]
    sin = sin_ref[...]
    main = jnp.dot(xh, wm_ref[...], preferred_element_type=F32)
    wih = wih_ref[...]
    idx = (jnp.dot(xh, wih, preferred_element_type=F32) + jnp.dot(xl, wih, preferred_element_type=F32)
           + jnp.dot(xh, wil_ref[...], preferred_element_type=F32))

    scale = HEAD_DIM ** -0.5 * LOG2_E
    for h in range(N_HEADS):
        blk = main[:, h * LANES:(h + 1) * LANES]
        q_ref[h * LANES:(h + 1) * LANES, :] = (_rope(blk, cos, sin) * scale).T.astype(BF16)
    k = _rope(main[:, QPAD:QPAD + KV_WIDTH], cos, sin)
    kf_ref[...] = k.T
    kb_ref[...] = k.astype(BF16)
    vt = main[:, QPAD + KV_WIDTH:QPAD + 2 * KV_WIDTH].T
    vf_ref[...] = vt
    vt_ref[:KV_WIDTH, :] = vt.astype(BF16)
    vt_ref[KV_WIDTH:, :] = jnp.ones((VT_ROWS - KV_WIDTH, vt.shape[1]), BF16)
    u_ref[...] = main[:, QPAD + 2 * KV_WIDTH:]

    for h in range(N_IDX_HEADS):
        d = _rope(idx[:, h * LANES:(h + 1) * LANES], cos, sin).T
        hi, lo = _split_bf16(d)
        qi4_ref[2 * h * LANES:(2 * h + 1) * LANES, :] = hi
        qi4_ref[(2 * h + 1) * LANES:(2 * h + 2) * LANES, :] = lo
    off = N_IDX_HEADS * LANES
    d = _rope(idx[:, off:off + LANES], cos, sin)
    kif_ref[...] = d.T[:IDX_DIM, :]
    hi = d.astype(BF16).astype(F32)
    lane = lax.broadcasted_iota(I32, d.shape, 1)
    hl = jnp.where(lane < IDX_DIM, hi, d - hi).astype(BF16)
    ki4_ref[:, :LANES] = hl
    ki4_ref[:, LANES:] = hl
    misc_ref[...] = idx[:, off + LANES:]


def _inproj(x, pos, g_pre, wm, wih, wil, tm):
    B, T, _ = x.shape
    half = HEAD_DIM // 2
    inv = ROPE_THETA ** (-jnp.arange(half, dtype=F32) / half)
    ang = pos.astype(F32)[:, None] * inv[None, :]
    cos = jnp.tile(jnp.cos(ang), (1, LANES // half))
    s = jnp.sin(ang)
    sin = jnp.tile(jnp.concatenate([-s, s], axis=1), (1, LANES // HEAD_DIM))

    def row(w):
        return pl.BlockSpec((None, tm, w), lambda b, t: (b, t, 0))

    def tab():
        return pl.BlockSpec((tm, LANES), lambda b, t: (t, 0))

    def col(w):
        return pl.BlockSpec((None, w, tm), lambda b, t: (b, 0, t))

    out_shapes = (
        jax.ShapeDtypeStruct((B, QPAD, T), BF16),
        jax.ShapeDtypeStruct((B, KV_WIDTH, T), F32),
        jax.ShapeDtypeStruct((B, T, KV_WIDTH), BF16),
        jax.ShapeDtypeStruct((B, KV_WIDTH, T), F32),
        jax.ShapeDtypeStruct((B, VT_ROWS, T), BF16),
        jax.ShapeDtypeStruct((T, B * SSM_WIDTH), F32),
        jax.ShapeDtypeStruct((B, IDX_DIM, T), F32),
        jax.ShapeDtypeStruct((B, T, 4 * IDX_DIM), BF16),
        jax.ShapeDtypeStruct((B, 4 * IDX_DIM * N_IDX_HEADS, T), BF16),
        jax.ShapeDtypeStruct((B, T, LANES), F32),
    )
    out_specs = (
        col(QPAD), col(KV_WIDTH), row(KV_WIDTH), col(KV_WIDTH), col(VT_ROWS),
        pl.BlockSpec((tm, SSM_WIDTH), lambda b, t: (t, b)),
        col(IDX_DIM), row(4 * IDX_DIM), col(4 * IDX_DIM * N_IDX_HEADS), row(LANES),
    )
    return pl.pallas_call(
        _inproj_kernel,
        grid=(B, T // tm),
        in_specs=[row(D_MODEL), _const_spec((1, D_MODEL)), _const_spec((D_MODEL, MAIN_W)),
                  _const_spec((D_MODEL, IDX_W)), _const_spec((D_MODEL, IDX_W)), tab(), tab()],
        out_specs=out_specs,
        out_shape=out_shapes,
        compiler_params=_cparams(2),
        name="inproj",
    )(x, g_pre.reshape(1, D_MODEL), wm, wih, wil, cos, sin)


def _inproj_weights(w_in):
    cols = []
    zero = jnp.zeros((D_MODEL, HEAD_DIM), F32)
    for h in range(N_HEADS):
        wq = w_in[:, OFF_Q + h * HEAD_DIM:OFF_Q + (h + 1) * HEAD_DIM]
        cols += [wq, zero] if h // Q_PER_KV == 0 else [zero, wq]
    cols += [w_in[:, OFF_K:OFF_QI], w_in[:, OFF_U:]]
    wm = jnp.concatenate(cols, axis=1).astype(BF16)
    icols = []
    for h in range(N_IDX_HEADS):
        wq = w_in[:, OFF_QI + h * IDX_DIM:OFF_QI + (h + 1) * IDX_DIM]
        icols += [wq, wq]
    wk = w_in[:, OFF_KI:OFF_WI]
    icols += [wk, wk, w_in[:, OFF_WI:OFF_U], jnp.zeros((D_MODEL, LANES - N_IDX_HEADS), F32)]
    wi = jnp.concatenate(icols, axis=1)
    wih = wi.astype(BF16)
    wil = (wi - wih.astype(F32)).astype(BF16)
    return wm, wih, wil


def _attn_kernel(qi4_ref, wit_ref, ki4_ref, q_ref, kb_ref, vt_ref, o_ref,
                 keys_ref, khi_ref, klo_ref, qs_ref, lg_ref, p_ref, alpha_ref, tie_ref,
                 m_ref, acc_ref, *, topk, tq):
    i = pl.program_id(1)
    nk = i + 1
    w = wit_ref[...]

    def chunk(c):
        return pl.ds(pl.multiple_of(c * tq, tq), tq)

    def score_chunk(c, diagonal):
        kc = ki4_ref[chunk(c), :]
        sc = None
        for h in range(N_IDX_HEADS):
            s = jnp.dot(kc, qi4_ref[4 * IDX_DIM * h:4 * IDX_DIM * (h + 1), :], preferred_element_type=F32)
            t = jnp.maximum(s, 0.0) * w[h:h + 1, :]
            sc = t if sc is None else sc + t
        key = _sortable(sc)
        if diagonal:
            kpos = lax.broadcasted_iota(I32, key.shape, 0)
            qpos = lax.broadcasted_iota(I32, key.shape, 1)
            key = jnp.where(kpos <= qpos, key, INT_MIN)
        keys_ref[chunk(c), :] = key
        khi_ref[chunk(c), :] = (key >> 16).astype(I16)

    def score_pair(cc, _):
        score_chunk(2 * cc, False)
        score_chunk(2 * cc + 1, False)
        return 0

    lax.fori_loop(0, i // 2, score_pair, 0)

    @pl.when(i % 2 == 1)
    def _():
        score_chunk(i - 1, False)

    score_chunk(i, True)

    pack_rows = 2 * SUBLANES
    n_acc = 4

    def count16(ref, cand):
        cand = cand.astype(I16)

        def add_rows(rows, accs):
            m = jnp.where(ref[rows, :] >= cand, jnp.ones((), I16), jnp.zeros((), I16))
            m = m.reshape(m.shape[0] // pack_rows, pack_rows, tq)
            accs = list(accs)
            for r in range(m.shape[0]):
                accs[r % n_acc] = accs[r % n_acc] + m[r]
            return tuple(accs)

        accs = tuple(jnp.zeros((pack_rows, tq), I16) for _ in range(n_acc))
        accs = lax.fori_loop(0, nk // 2, lambda c, a: add_rows(pl.ds(pl.multiple_of(c * 2 * tq, 2 * tq), 2 * tq), a),
                             accs)
        accs = lax.fori_loop(2 * (nk // 2), nk, lambda c, a: add_rows(chunk(c), a), accs)
        tot = (accs[0] + accs[1]) + (accs[2] + accs[3])
        return jnp.sum(tot.astype(I32), axis=0, keepdims=True)

    def search16(ref):
        def bit_body(b, t):
            cand = t + (jnp.int32(1) << (15 - b))
            return jnp.where(count16(ref, cand) >= topk, cand, t)
        return lax.fori_loop(0, 16, bit_body, jnp.full((1, tq), I16_MIN, I32))

    tau_hi = search16(khi_ref)

    def lo_body(c, _):
        k = keys_ref[chunk(c), :]
        hi = k >> 16
        lo = (k & 0xFFFF) + I16_MIN
        lo = jnp.where(hi == tau_hi, lo, jnp.where(hi > tau_hi, I16_MAX, I16_MIN))
        klo_ref[chunk(c), :] = lo.astype(I16)
        return 0

    lax.fori_loop(0, nk, lo_body, 0)
    tau_lo = search16(klo_ref)
    tau = (tau_hi << 16) + (tau_lo - I16_MIN)
    tau = jnp.maximum(tau, INT_MIN + 1)

    def gt_body(c, acc):
        m = jnp.where(keys_ref[chunk(c), :] > tau, 1, 0).astype(I32)
        return acc + jnp.sum(m.reshape(tq // SUBLANES, SUBLANES, tq), axis=0)

    n_gt = jnp.sum(lax.fori_loop(0, nk, gt_body, jnp.zeros((SUBLANES, tq), I32)), axis=0, keepdims=True)
    quota = (topk - n_gt).astype(F32)

    for h in range(N_HEADS):
        qs_ref[:, h * tq:(h + 1) * tq] = q_ref[h * LANES:(h + 1) * LANES, :]
    m_ref[...] = jnp.full(m_ref.shape, NEG, F32)
    acc_ref[...] = jnp.zeros(acc_ref.shape, F32)
    r_i = lax.broadcasted_iota(I32, (tq, tq), 0)
    c_i = lax.broadcasted_iota(I32, (tq, tq), 1)
    ltri = jnp.where(c_i <= r_i, 1.0, 0.0).astype(BF16)

    def prepare(c, slot):
        cc = jnp.minimum(c, nk - 1)
        kc = keys_ref[chunk(cc), :]
        eq = kc == tau
        e = jnp.where(eq, 1.0, 0.0)
        pre = jnp.dot(ltri, e.astype(BF16), preferred_element_type=F32)
        rank = tie_ref[...] + pre - e
        sel = ((kc > tau) | (eq & (rank < quota))) & (c < nk)
        tie_ref[...] = tie_ref[...] + pre[tq - 1:tq, :]
        bias = jnp.where(sel, 0.0, NEG)
        lg = jnp.dot(kb_ref[chunk(cc), :], qs_ref[...], preferred_element_type=F32)
        for h in range(N_HEADS):
            lg_ref[slot, :, h * tq:(h + 1) * tq] = (lg[:, h * tq:(h + 1) * tq] + bias).astype(BF16)

    def softmax(slot):
        for j in range(N_HEADS * tq // LANES):
            cols = slice(j * LANES, (j + 1) * LANES)
            x = lg_ref[slot, :, cols]
            m_prev = m_ref[:, cols]
            m_new = jnp.maximum(m_prev, jnp.max(x, axis=0, keepdims=True).astype(F32))
            alpha = jnp.exp2(m_prev - m_new)
            m_ref[:, cols] = m_new
            p_ref[slot, :, cols] = jnp.exp2(jnp.maximum(x - m_new.astype(BF16), EXP2_FLOOR))
            alpha_ref[slot, :, cols] = alpha

    def value(c, slot):
        cc = jnp.clip(c, 0, nk - 1)
        acc_ref[...] = alpha_ref[slot] * acc_ref[...] + jnp.dot(vt_ref[:, chunk(cc)], p_ref[slot],
                                                                preferred_element_type=F32)

    tie_ref[...] = jnp.zeros(tie_ref.shape, F32)
    p_ref[1] = jnp.zeros(p_ref.shape[1:], BF16)
    alpha_ref[1] = jnp.ones(alpha_ref.shape[1:], F32)
    prepare(0, 0)

    def pair_body(cc, _):
        for slot in range(2):
            c = 2 * cc + slot
            prepare(c + 1, 1 - slot)
            softmax(slot)
            value(c - 1, 1 - slot)
        return 0

    n_pairs = (nk + 1) // 2
    lax.fori_loop(0, n_pairs, pair_body, 0)
    value(2 * n_pairs - 1, 1)
    out = acc_ref[:KV_WIDTH, :] / acc_ref[KV_WIDTH:KV_WIDTH + 1, :]
    for h in range(N_HEADS):
        o_ref[:, h * LANES:(h + 1) * LANES] = out[:, h * tq:(h + 1) * tq].T.astype(BF16)


def _prompt_attention(qi4, wit, ki4, q, kb, vt, tq):
    B, T, _ = kb.shape
    topk = min(TOPK_MAX, T // 4)

    def tile(w):
        return pl.BlockSpec((None, tq, w), lambda b, i: (b, i, 0))

    def full(w):
        return pl.BlockSpec((None, T, w), lambda b, i: (b, 0, 0))

    return pl.pallas_call(
        functools.partial(_attn_kernel, topk=topk, tq=tq),
        grid=(B, T // tq),
        in_specs=[pl.BlockSpec((None, 4 * IDX_DIM * N_IDX_HEADS, tq), lambda b, i: (b, 0, i)),
                  pl.BlockSpec((None, N_IDX_HEADS, tq), lambda b, i: (b, 0, i)),
                  full(4 * IDX_DIM), pl.BlockSpec((None, QPAD, tq), lambda b, i: (b, 0, i)), full(KV_WIDTH),
                  pl.BlockSpec((None, VT_ROWS, T), lambda b, i: (b, 0, 0))],
        out_specs=tile(QPAD),
        out_shape=jax.ShapeDtypeStruct((B, T, QPAD), BF16),
        scratch_shapes=[pltpu.VMEM((T, tq), I32), pltpu.VMEM((T, tq), I16), pltpu.VMEM((T, tq), I16),
                        pltpu.VMEM((LANES, N_HEADS * tq), BF16), pltpu.VMEM((2, tq, N_HEADS * tq), BF16),
                        pltpu.VMEM((2, tq, N_HEADS * tq), BF16),
                        pltpu.VMEM((2, 1, N_HEADS * tq), F32), pltpu.VMEM((1, tq), F32),
                        pltpu.VMEM((1, N_HEADS * tq), F32), pltpu.VMEM((VT_ROWS, N_HEADS * tq), F32)],
        compiler_params=_cparams(2),
        name="prompt_attention",
    )(qi4, wit, ki4, q, kb, vt)


def _s5_prep_kernel(lr_ref, li_ref, ldt_ref, brt_ref, bit_ref, ar_ref, ai_ref, bbr_ref, bbi_ref):
    lr = lr_ref[...]
    li = li_ref[...]
    dt = jnp.exp(ldt_ref[...])
    mag = jnp.exp(lr * dt)
    ar = mag * jnp.cos(li * dt)
    ai = mag * jnp.sin(li * dt)
    den = lr * lr + li * li
    cr = ((ar - 1.0) * lr + ai * li) / den
    ci = (ai * lr - (ar - 1.0) * li) / den
    ar_ref[...] = ar
    ai_ref[...] = ai
    brt = brt_ref[...]
    bit = bit_ref[...]
    bbr_ref[...] = cr[:, None, :] * brt - ci[:, None, :] * bit
    bbi_ref[...] = cr[:, None, :] * bit + ci[:, None, :] * brt


def _s5_weights(lambda_re, lambda_im, log_dt, b_re, b_im, c_re, c_im):
    G, P, C = N_SSM_GROUPS, SSM_STATE, SSM_GROUP
    ar, ai, bbr, bbi = pl.pallas_call(
        _s5_prep_kernel,
        out_shape=(jax.ShapeDtypeStruct((G, P), F32), jax.ShapeDtypeStruct((G, P), F32),
                   jax.ShapeDtypeStruct((G, C, P), F32), jax.ShapeDtypeStruct((G, C, P), F32)),
        name="s5_prep",
    )(lambda_re, lambda_im, log_dt.reshape(G, 1), jnp.swapaxes(b_re, 1, 2), jnp.swapaxes(b_im, 1, 2))
    eye = jnp.eye(GROUPS_PER_CHUNK, dtype=F32)

    def in_blocks(bb):
        x = bb.reshape(N_UCHUNK, GROUPS_PER_CHUNK, C, P)
        x = x[:, :, :, None, :] * eye[None, :, None, :, None]
        return x.reshape(N_UCHUNK, LANES, GROUPS_PER_CHUNK * P)

    def out_blocks(cc):
        x = jnp.swapaxes(cc, 1, 2).reshape(N_UCHUNK, GROUPS_PER_CHUNK, P, C)
        x = x[:, :, :, None, :] * eye[None, :, None, :, None]
        return x.reshape(N_UCHUNK, GROUPS_PER_CHUNK * P, LANES)

    bblk = jnp.concatenate([in_blocks(bbr), in_blocks(bbi)], axis=2).astype(BF16)
    cblk = jnp.concatenate([out_blocks(c_re), out_blocks(-c_im)], axis=1).astype(BF16)
    half = SLABS_PER_CHUNK // 2

    def slabs(a):
        x = a.reshape(N_UCHUNK, 1, half, 1, LANES)
        return jnp.broadcast_to(x, (N_UCHUNK, 2, half, 1, LANES)).reshape(N_SLAB, 1, LANES)

    return slabs(ar), slabs(ai), bblk, cblk


def _state_to_slabs(h_re, h_im):
    B = h_re.shape[0]
    half = SLABS_PER_CHUNK // 2
    re = h_re.reshape(B, N_UCHUNK, 1, half, LANES)
    im = h_im.reshape(B, N_UCHUNK, 1, half, LANES)
    x = jnp.concatenate([re, im], axis=2).reshape(B, N_SLAB, LANES)
    return jnp.swapaxes(x, 0, 1)


def _slabs_to_state(h):
    B = h.shape[1]
    half = SLABS_PER_CHUNK // 2
    x = jnp.swapaxes(h, 0, 1).reshape(B, N_UCHUNK, 2, half * LANES)
    re = x[:, :, 0].reshape(B, N_SSM_GROUPS, SSM_STATE)
    im = x[:, :, 1].reshape(B, N_SSM_GROUPS, SSM_STATE)
    return re, im


def _s5_kernel(u_ref, h0_ref, ar_ref, ai_ref, bblk_ref, cblk_ref, d_ref, y_ref, hout_ref, bu_ref, hs_ref,
               *, nb, tj, lane_batches):
    rows = tj * nb
    half = SLABS_PER_CHUNK // 2
    blk_rows = rows // lane_batches

    @pl.when(pl.program_id(0) == 0)
    def _():
        hout_ref[...] = h0_ref[...]

    def batch_rows(lb):
        if lane_batches == 1:
            return pl.ds(0, blk_rows)
        return pl.ds(lb, blk_rows, stride=lane_batches)

    def u_cols(lb, q):
        return slice(lb * SSM_WIDTH + q * LANES, lb * SSM_WIDTH + (q + 1) * LANES)

    for q in range(N_UCHUNK):
        ub = jnp.concatenate([u_ref[:, u_cols(lb, q)] for lb in range(lane_batches)], axis=0).astype(BF16)
        bu = jnp.dot(ub, bblk_ref[q], preferred_element_type=F32)
        for lb in range(lane_batches):
            for s in range(SLABS_PER_CHUNK):
                bu_ref[q * SLABS_PER_CHUNK + s, batch_rows(lb), :] = bu[lb * blk_rows:(lb + 1) * blk_rows,
                                                                        s * LANES:(s + 1) * LANES]

    def step(j, h):
        cur = pl.ds(j * nb, nb)
        new = list(h)
        for q in range(N_UCHUNK):
            for r in range(half):
                sr = q * SLABS_PER_CHUNK + r
                si = sr + half
                a_re = ar_ref[sr]
                a_im = ai_ref[sr]
                new[sr] = a_re * h[sr] - a_im * h[si] + bu_ref[sr, cur, :]
                new[si] = a_re * h[si] + a_im * h[sr] + bu_ref[si, cur, :]
                hs_ref[sr, cur, :] = new[sr]
                hs_ref[si, cur, :] = new[si]
        return tuple(new)

    h = lax.fori_loop(0, tj, step, tuple(hout_ref[s] for s in range(N_SLAB)))
    for s in range(N_SLAB):
        hout_ref[s] = h[s]

    for q in range(N_UCHUNK):
        hq = jnp.concatenate(
            [jnp.concatenate([hs_ref[q * SLABS_PER_CHUNK + s, batch_rows(lb), :] for s in range(SLABS_PER_CHUNK)],
                             axis=1) for lb in range(lane_batches)], axis=0).astype(BF16)
        yq = jnp.dot(hq, cblk_ref[q], preferred_element_type=F32)
        for lb in range(lane_batches):
            cols = u_cols(lb, q)
            y_ref[:, cols] = (yq[lb * blk_rows:(lb + 1) * blk_rows]
                              + d_ref[:, q * LANES:(q + 1) * LANES] * u_ref[:, cols])


def _s5_scan(u, nb, lane_batches, h0_slabs, weights, d_skip, tj):
    T = u.shape[0] * lane_batches // nb
    ar, ai, bblk, cblk = weights
    rows = tj * nb
    blk = (rows // lane_batches, lane_batches * SSM_WIDTH)
    y, hout = pl.pallas_call(
        functools.partial(_s5_kernel, nb=nb, tj=tj, lane_batches=lane_batches),
        grid=(T // tj,),
        in_specs=[pl.BlockSpec(blk, lambda t: (t, 0)),
                  _const_spec((N_SLAB, nb, LANES)), _const_spec((N_SLAB, 1, LANES)),
                  _const_spec((N_SLAB, 1, LANES)),
                  _const_spec((N_UCHUNK, LANES, SLABS_PER_CHUNK * LANES)),
                  _const_spec((N_UCHUNK, SLABS_PER_CHUNK * LANES, LANES)),
                  _const_spec((1, SSM_WIDTH))],
        out_specs=(pl.BlockSpec(blk, lambda t: (t, 0)),
                   pl.BlockSpec((N_SLAB, nb, LANES), lambda t: (0, 0, 0))),
        out_shape=(jax.ShapeDtypeStruct(u.shape, F32),
                   jax.ShapeDtypeStruct((N_SLAB, nb, LANES), F32)),
        scratch_shapes=[pltpu.VMEM((N_SLAB, rows, LANES), F32), pltpu.VMEM((N_SLAB, rows, LANES), F32)],
        compiler_params=_cparams(1),
        name="s5_scan",
    )(u, h0_slabs, ar, ai, bblk, cblk, d_skip.reshape(1, SSM_WIDTH))
    return y, hout


def _post_kernel(x_ref, ao_ref, y_ref, wglu_ref, bglu_ref, woa_ref, wos_ref, gpost_ref, gpre_ref,
                 wup_ref, wdn_ref, gfpost_ref, o_ref):
    z = jax.nn.gelu(y_ref[...])
    gate = jax.nn.sigmoid(jnp.dot(z.astype(BF16), wglu_ref[...], preferred_element_type=F32) + bglu_ref[...])
    so = z * gate
    mix = (jnp.dot(ao_ref[...], woa_ref[...], preferred_element_type=F32)
           + jnp.dot(so.astype(BF16), wos_ref[...], preferred_element_type=F32))
    x = x_ref[...] + _rms(mix, gpost_ref[...])
    hdn = jnp.dot(_rms(x, gpre_ref[...]).astype(BF16), wup_ref[...], preferred_element_type=F32)
    hdn = jnp.square(jnp.maximum(hdn, 0.0))
    dn = jnp.dot(hdn.astype(BF16), wdn_ref[...], preferred_element_type=F32)
    o_ref[...] = x + _rms(dn, gfpost_ref[...])


def _post_weights(w_glu, w_out, w_up, w_down):
    zero = jnp.zeros((HEAD_DIM, D_MODEL), F32)
    rows = []
    for h in range(N_HEADS):
        wh = w_out[h * HEAD_DIM:(h + 1) * HEAD_DIM]
        rows += [wh, zero] if h // Q_PER_KV == 0 else [zero, wh]
    woa = jnp.concatenate(rows, axis=0).astype(BF16)
    return (w_glu.astype(BF16), woa, w_out[ATTN_WIDTH:].astype(BF16), w_up.astype(BF16), w_down.astype(BF16))


def _post(x, ao, y_tb, weights, b_glu, g_post, g_ffn_pre, g_ffn_post, tm):
    B, T, _ = x.shape
    wglu, woa, wos, wup, wdn = weights

    def row(w):
        return pl.BlockSpec((None, tm, w), lambda b, t: (b, t, 0))

    def vec(a):
        return a.reshape(1, -1)

    return pl.pallas_call(
        _post_kernel,
        grid=(B, T // tm),
        in_specs=[row(D_MODEL), row(QPAD), pl.BlockSpec((tm, SSM_WIDTH), lambda b, t: (t, b)),
                  _const_spec((SSM_WIDTH, SSM_WIDTH)), _const_spec((1, SSM_WIDTH)),
                  _const_spec((QPAD, D_MODEL)), _const_spec((SSM_WIDTH, D_MODEL)),
                  _const_spec((1, D_MODEL)), _const_spec((1, D_MODEL)),
                  _const_spec((D_MODEL, D_FF)), _const_spec((D_FF, D_MODEL)), _const_spec((1, D_MODEL))],
        out_specs=row(D_MODEL),
        out_shape=jax.ShapeDtypeStruct((B, T, D_MODEL), F32),
        compiler_params=_cparams(2),
        name="post",
    )(x, ao, y_tb, wglu, vec(b_glu), woa, wos, vec(g_post), vec(g_ffn_pre), wup, wdn, vec(g_ffn_post))


def _page_map(b, j, pt, *, r, n):
    return (pt[b, j * n + r], 0, 0)


def _skeys_kernel(pt_ref, qh_ref, ql_ref, w_ref, kn_ref, *rest, n_pages_step, dec_t):
    page_refs = rest[:n_pages_step]
    keys_ref, knew_ref = rest[n_pages_step:]
    j = pl.program_id(1)
    qh = qh_ref[...]
    ql = ql_ref[...]
    w = w_ref[...]

    def scores(kc):
        kh, kl = _split_bf16(kc)
        s = (jnp.dot(qh, kh, preferred_element_type=F32) + jnp.dot(ql, kh, preferred_element_type=F32)
             + jnp.dot(qh, kl, preferred_element_type=F32))
        sc = None
        for h in range(N_IDX_HEADS):
            t = jnp.maximum(s[h * dec_t:(h + 1) * dec_t], 0.0) * w[:, h:h + 1]
            sc = t if sc is None else sc + t
        return _sortable(sc)

    keys_ref[...] = scores(jnp.concatenate([r[...] for r in page_refs], axis=1))

    @pl.when(j == pl.num_programs(1) - 1)
    def _():
        kn = scores(kn_ref[...])
        qi_ = lax.broadcasted_iota(I32, kn.shape, 0)
        ki_ = lax.broadcasted_iota(I32, kn.shape, 1)
        knew_ref[...] = jnp.where(ki_ <= qi_, kn, INT_MIN)


def _ssearch_kernel(kp_ref, kn_ref, sel_ref, *, topk, past_len):
    def count(pred_past, pred_new):
        c = jnp.sum(jnp.where(pred_past, 1, 0).astype(I32), axis=1, keepdims=True)
        return c + jnp.sum(jnp.where(pred_new, 1, 0).astype(I32), axis=1, keepdims=True)

    def count_ge(cand):
        return count(kp_ref[...] >= cand, kn_ref[...] >= cand)

    rows = kp_ref.shape[0]
    tau = jnp.where(count_ge(jnp.zeros((rows, 1), I32)) >= topk, 0, INT_MIN).astype(I32)

    def bit_body(b, tau):
        cand = tau + (jnp.int32(1) << (30 - b))
        return jnp.where(count_ge(cand) >= topk, cand, tau)

    tau = lax.fori_loop(0, 31, bit_body, tau)
    tau = jnp.maximum(tau, INT_MIN + 1)
    quota = topk - count(kp_ref[...] > tau, kn_ref[...] > tau)
    pos_p = lax.broadcasted_iota(I32, kp_ref.shape, 1)
    pos_n = past_len + lax.broadcasted_iota(I32, kn_ref.shape, 1)
    n_bits = max(1, (past_len + kn_ref.shape[1] - 1).bit_length())

    def idx_body(b, lim):
        cand = lim + (jnp.int32(1) << (n_bits - 1 - b))
        ties = count((kp_ref[...] == tau) & (pos_p < cand), (kn_ref[...] == tau) & (pos_n < cand))
        return jnp.where(ties < quota, cand, lim)

    lim = lax.fori_loop(0, n_bits, idx_body, jnp.zeros_like(tau))
    kp = kp_ref[...]
    kn = kn_ref[...]
    sel_ref[:, :past_len] = jnp.where((kp > tau) | ((kp == tau) & (pos_p <= lim)), 1.0, 0.0)
    sel_ref[:, past_len:] = jnp.where((kn > tau) | ((kn == tau) & (pos_n <= lim)), 1.0, 0.0)


def _sample_select(page_table, qh, ql, w, ki_new_pad, cache_kidx, n_pages_step):
    DB, n_pages = page_table.shape
    dec_t = w.shape[1]
    past_len = n_pages * PAGE_SIZE
    lp = past_len + LANES
    topk = min(TOPK_MAX, (past_len + dec_t) // 4)
    rows = N_IDX_HEADS * dec_t
    step_keys = n_pages_step * PAGE_SIZE

    def fixed(shape):
        return pl.BlockSpec((None,) + shape, lambda b, j, pt: (b, 0, 0))

    pages = [pl.BlockSpec((None, IDX_DIM, PAGE_SIZE), functools.partial(_page_map, r=r, n=n_pages_step))
             for r in range(n_pages_step)]
    grid_spec = pltpu.PrefetchScalarGridSpec(
        num_scalar_prefetch=1,
        grid=(DB, n_pages // n_pages_step),
        in_specs=[fixed((rows, IDX_DIM)), fixed((rows, IDX_DIM)), fixed((dec_t, N_IDX_HEADS)),
                  fixed((IDX_DIM, LANES))] + pages,
        out_specs=(pl.BlockSpec((None, dec_t, step_keys), lambda b, j, pt: (b, 0, j)), fixed((dec_t, LANES))),
    )
    keys_past, keys_new = pl.pallas_call(
        functools.partial(_skeys_kernel, n_pages_step=n_pages_step, dec_t=dec_t),
        grid_spec=grid_spec,
        out_shape=(jax.ShapeDtypeStruct((DB, dec_t, past_len), I32), jax.ShapeDtypeStruct((DB, dec_t, LANES), I32)),
        compiler_params=_cparams(2),
        name="sample_keys",
    )(page_table, qh, ql, w, ki_new_pad, *([cache_kidx] * n_pages_step))
    sel = pl.pallas_call(
        functools.partial(_ssearch_kernel, topk=topk, past_len=past_len),
        out_shape=jax.ShapeDtypeStruct((DB * dec_t, lp), F32),
        compiler_params=pltpu.CompilerParams(vmem_limit_bytes=VMEM_LIMIT),
        name="sample_search",
    )(keys_past.reshape(DB * dec_t, past_len), keys_new.reshape(DB * dec_t, LANES))
    return sel.reshape(DB, dec_t, lp)


def _sattn_kernel(pt_ref, sel_ref, q_ref, kn_ref, vn_ref, *rest, n_pages_step, past_len, dec_t):
    k_refs = rest[:n_pages_step]
    v_refs = rest[n_pages_step:2 * n_pages_step]
    o_ref, m_ref, l_ref, acc_ref = rest[2 * n_pages_step:]
    j = pl.program_id(1)
    step_keys = n_pages_step * PAGE_SIZE
    q = q_ref[...]

    @pl.when(j == 0)
    def _():
        m_ref[...] = jnp.full(m_ref.shape, NEG, F32)
        l_ref[...] = jnp.zeros(l_ref.shape, F32)
        acc_ref[...] = jnp.zeros(acc_ref.shape, F32)

    def update(kc, vc, sel):
        n = kc.shape[1]
        lg = jnp.dot(q, kc, preferred_element_type=F32)
        lg = jnp.where(sel[None] > 0.5, lg.reshape(N_HEADS, dec_t, n), NEG).reshape(N_HEADS * dec_t, n)
        m_prev = m_ref[...]
        m_new = jnp.maximum(m_prev, jnp.max(lg, axis=1, keepdims=True))
        alpha = jnp.exp2(m_prev - m_new)
        p = jnp.exp2(lg - m_new)
        l_ref[...] = alpha * l_ref[...] + jnp.sum(p, axis=1, keepdims=True)
        acc_ref[...] = alpha * acc_ref[...] + lax.dot_general(p.astype(BF16), vc, NT_DIMS,
                                                              preferred_element_type=F32)
        m_ref[...] = m_new

    kc = jnp.concatenate([r[...] for r in k_refs], axis=1).astype(BF16)
    vc = jnp.concatenate([r[...] for r in v_refs], axis=1).astype(BF16)
    update(kc, vc, sel_ref[:, pl.ds(pl.multiple_of(j * step_keys, step_keys), step_keys)])

    @pl.when(j == pl.num_programs(1) - 1)
    def _():
        update(kn_ref[...], vn_ref[...], sel_ref[:, past_len:])
        o_ref[...] = acc_ref[...] / l_ref[...]


def _sample_attention(page_table, sel, q, k_new_pad, v_new_pad, cache_k, cache_v, n_pages_step):
    DB, n_pages = page_table.shape
    dec_t = sel.shape[1]
    past_len = n_pages * PAGE_SIZE
    lp = past_len + LANES
    rows = N_HEADS * dec_t

    def fixed(shape):
        return pl.BlockSpec((None,) + shape, lambda b, j, pt: (b, 0, 0))

    def pages():
        return [pl.BlockSpec((None, KV_WIDTH, PAGE_SIZE), functools.partial(_page_map, r=r, n=n_pages_step))
                for r in range(n_pages_step)]

    grid_spec = pltpu.PrefetchScalarGridSpec(
        num_scalar_prefetch=1,
        grid=(DB, n_pages // n_pages_step),
        in_specs=[fixed((dec_t, lp)), fixed((rows, KV_WIDTH)), fixed((KV_WIDTH, LANES)),
                  fixed((KV_WIDTH, LANES))] + pages() + pages(),
        out_specs=fixed((rows, KV_WIDTH)),
        scratch_shapes=[pltpu.VMEM((rows, 1), F32), pltpu.VMEM((rows, 1), F32),
                        pltpu.VMEM((rows, KV_WIDTH), F32)],
    )
    return pl.pallas_call(
        functools.partial(_sattn_kernel, n_pages_step=n_pages_step, past_len=past_len, dec_t=dec_t),
        grid_spec=grid_spec,
        out_shape=jax.ShapeDtypeStruct((DB, rows, KV_WIDTH), F32),
        compiler_params=_cparams(2),
        name="sample_attention",
    )(page_table, sel, q, k_new_pad, v_new_pad, *([cache_k] * n_pages_step), *([cache_v] * n_pages_step))


def _largest_tile(n, cap):
    t = min(n, cap)
    while n % t:
        t //= 2
    return t


def _layer(hp, hs, cache_k, cache_v, cache_kidx, st_re, st_im, page_table, g_pre, w_in, ssm_params, d_skip,
           w_glu, b_glu, w_out, g_post, g_ffn_pre, w_up, w_down, g_ffn_post):
    B, T, _ = hp.shape
    DB, dec_t, _ = hs.shape
    n_pages = page_table.shape[1]
    past_len = n_pages * PAGE_SIZE
    idx_scale = (IDX_DIM ** -0.5) * (N_IDX_HEADS ** -0.5)

    wm, wih, wil = _inproj_weights(w_in)
    s5w = _s5_weights(*ssm_params)
    postw = _post_weights(w_glu, w_out, w_up, w_down)

    pos_p = jnp.arange(T, dtype=I32)
    q, kft, kb, vft, vt, u_tb, kift, ki4, qi4, misc = _inproj(hp, pos_p, g_pre, wm, wih, wil, _largest_tile(T, 512))
    wit = jnp.swapaxes(misc[:, :, :N_IDX_HEADS], 1, 2) * idx_scale
    ao = _prompt_attention(qi4, wit, ki4, q, kb, vt, _largest_tile(T, 256))
    h0 = jnp.zeros((N_SLAB, B, LANES), F32)
    y_tb, hfin = _s5_scan(u_tb, B, B, h0, s5w, d_skip, _largest_tile(T, 256))
    yp = _post(hp, ao, y_tb, postw, b_glu, g_post, g_ffn_pre, g_ffn_post, _largest_tile(T, 512))
    srp, sip = _slabs_to_state(hfin)

    def kv_rows(a, nb, t):
        return jnp.transpose(a.reshape(nb, N_KV_HEADS, HEAD_DIM, t), (0, 3, 1, 2))

    prompt_out = (yp, kv_rows(kft, B, T), kv_rows(vft, B, T), jnp.swapaxes(kift, 1, 2), srp, sip)

    n_s = DB * dec_t
    pos_s = jnp.tile(past_len + jnp.arange(dec_t, dtype=I32), DB)
    q, kft, _, vft, _, u_s, kift, _, qi4, misc = _inproj(hs.reshape(1, n_s, D_MODEL), pos_s, g_pre, wm, wih, wil,
                                                        _largest_tile(n_s, 256))
    qi4 = jnp.transpose(qi4.reshape(N_IDX_HEADS, 4 * IDX_DIM, DB, dec_t), (2, 0, 3, 1))

    def heads_first(a):
        return a.reshape(DB, a.shape[1] * dec_t, a.shape[3])

    def per_batch(a):
        return jnp.swapaxes(a.reshape(a.shape[1], DB, dec_t), 0, 1)

    def pad_new(a, dtype):
        return jnp.pad(a, ((0, 0), (0, 0), (0, LANES - dec_t))).astype(dtype)

    k_s, v_s, ki_s = per_batch(kft), per_batch(vft), per_batch(kift)
    qh = heads_first(qi4[..., :IDX_DIM])
    ql = heads_first(qi4[..., 2 * IDX_DIM:3 * IDX_DIM])
    w_s = misc.reshape(DB, dec_t, LANES)[:, :, :N_IDX_HEADS] * idx_scale

    n_phys = cache_k.shape[0]
    pool_ki = jnp.swapaxes(cache_kidx, 1, 2)
    pool_k = jnp.transpose(cache_k, (0, 2, 3, 1)).reshape(n_phys, KV_WIDTH, PAGE_SIZE)
    pool_v = jnp.transpose(cache_v, (0, 2, 3, 1)).reshape(n_phys, KV_WIDTH, PAGE_SIZE)
    n_step = _largest_tile(n_pages, 32)
    sel = _sample_select(page_table, qh, ql, w_s, pad_new(ki_s, F32), pool_ki, n_step)
    qs = jnp.transpose(q.reshape(N_HEADS, LANES, DB, dec_t), (2, 0, 3, 1)).reshape(DB, N_HEADS * dec_t, LANES)
    ao_s = _sample_attention(page_table, sel, qs, pad_new(k_s, BF16), pad_new(v_s, BF16), pool_k, pool_v, n_step)
    ao_s = jnp.swapaxes(ao_s.reshape(DB, N_HEADS, dec_t, LANES), 1, 2).reshape(1, n_s, QPAD).astype(BF16)
    u_t = jnp.swapaxes(u_s.reshape(DB, dec_t, SSM_WIDTH), 0, 1).reshape(n_s, SSM_WIDTH)
    y_t, hfin = _s5_scan(u_t, DB, 1, _state_to_slabs(st_re, st_im), s5w, d_skip, dec_t)
    y_s = jnp.swapaxes(y_t.reshape(dec_t, DB, SSM_WIDTH), 0, 1).reshape(n_s, SSM_WIDTH)
    ys = _post(hs.reshape(1, n_s, D_MODEL), ao_s, y_s, postw, b_glu, g_post, g_ffn_pre, g_ffn_post,
               _largest_tile(n_s, 256))
    srs, sis = _slabs_to_state(hfin)
    sample_out = (ys.reshape(DB, dec_t, D_MODEL), kv_rows(k_s, DB, dec_t), kv_rows(v_s, DB, dec_t),
                  jnp.swapaxes(ki_s, 1, 2), srs, sis)
    return prompt_out, sample_out


def kernel(x_prompt, x_sample, cache_k, cache_v, cache_kidx, state_ssm_re, state_ssm_im, page_table, norm_mix_pre, w_in, lambda_re, lambda_im, log_dt, b_re, b_im, c_re, c_im, d_skip, w_glu, b_glu, w_out, norm_mix_post, norm_ffn_pre, w_up, w_down, norm_ffn_post):
    depth = w_in.shape[0]
    hp, hs = x_prompt, x_sample
    p_acc = [[] for _ in range(5)]
    s_acc = [[] for _ in range(5)]
    for l in range(depth):
        ssm_params = (lambda_re[l], lambda_im[l], log_dt[l], b_re[l], b_im[l], c_re[l], c_im[l])
        p_out, s_out = _layer(hp, hs, cache_k[l], cache_v[l], cache_kidx[l], state_ssm_re[l], state_ssm_im[l],
                              page_table, norm_mix_pre[l], w_in[l], ssm_params, d_skip[l], w_glu[l], b_glu[l],
                              w_out[l], norm_mix_post[l], norm_ffn_pre[l], w_up[l], w_down[l], norm_ffn_post[l])
        hp, hs = p_out[0], s_out[0]
        for acc, vals in ((p_acc, p_out[1:]), (s_acc, s_out[1:])):
            for a, v in zip(acc, vals):
                a.append(v)
    return (hp, hs) + tuple(jnp.stack(a) for a in p_acc) + tuple(jnp.stack(a) for a in s_acc)
```

```python
import functools
import math

import jax
import jax.numpy as jnp
from jax import lax
from jax.experimental import pallas as pl
from jax.experimental.pallas import tpu as pltpu

F32 = jnp.float32
BF16 = jnp.bfloat16
I32 = jnp.int32

D_MODEL = 1024
PAGE_SIZE = 128
HEAD_DIM = 64
ATTN_WIDTH = D_MODEL // 2
N_HEADS = ATTN_WIDTH // HEAD_DIM
N_KV_HEADS = 2
Q_PER_KV = N_HEADS // N_KV_HEADS
KV_WIDTH = N_KV_HEADS * HEAD_DIM
N_IDX_HEADS = 4
IDX_DIM = 64
TOPK_MAX = 256
ROPE_THETA = 10000.0
SSM_WIDTH = D_MODEL - ATTN_WIDTH
SSM_GROUP = 16
N_SSM_GROUPS = SSM_WIDTH // SSM_GROUP
SSM_STATE = 64
D_FF = 4 * D_MODEL
RMS_EPS = 1e-6
NEG = -1e30
OFF_Q = 0
OFF_K = OFF_Q + ATTN_WIDTH
OFF_V = OFF_K + KV_WIDTH
OFF_QI = OFF_V + KV_WIDTH
OFF_KI = OFF_QI + N_IDX_HEADS * IDX_DIM
OFF_WI = OFF_KI + IDX_DIM
OFF_U = OFF_WI + N_IDX_HEADS
IN_WIDTH = OFF_U + SSM_WIDTH

LANES = 128
SUBLANES = 8
INT_MIN = -(2 ** 31)
VMEM_LIMIT = 56 * 1024 * 1024

QPAD = N_HEADS * LANES
MAIN_W = QPAD + 2 * KV_WIDTH + SSM_WIDTH
IDX_W = 2 * N_IDX_HEADS * IDX_DIM + 2 * IDX_DIM + LANES
SSM_LANES = N_SSM_GROUPS * SSM_STATE
N_SLAB = 2 * SSM_LANES // LANES
N_UCHUNK = SSM_WIDTH // LANES
GROUPS_PER_CHUNK = LANES // SSM_GROUP
SLABS_PER_CHUNK = N_SLAB // N_UCHUNK

NT_DIMS = (((1,), (1,)), ((), ()))
LOG2_E = math.log2(math.e)
I16 = jnp.int16
I16_MIN = -(2 ** 15)
I16_MAX = 2 ** 15 - 1
VT_ROWS = KV_WIDTH + 2 * SUBLANES


def _cparams(n_grid):
    return pltpu.CompilerParams(dimension_semantics=("arbitrary",) * n_grid,
                                vmem_limit_bytes=VMEM_LIMIT)


def _const_spec(shape):
    zeros = (0,) * len(shape)
    return pl.BlockSpec(shape, lambda *_: zeros, pipeline_mode=pl.Buffered(1))


def _rms(x, g):
    return x * lax.rsqrt(jnp.mean(x * x, axis=-1, keepdims=True) + RMS_EPS) * g


def _split_bf16(x):
    hi = x.astype(BF16)
    lo = (x - hi.astype(F32)).astype(BF16)
    return hi, lo


def _sortable(x):
    b = lax.bitcast_convert_type(x, I32)
    return b ^ ((b >> 31) & jnp.int32(0x7FFFFFFF))


def _rope(x, cos, sin):
    w = x.shape[-1]
    lane = lax.broadcasted_iota(I32, x.shape, 1)
    first = (lane & (HEAD_DIM - 1)) < HEAD_DIM // 2
    sw = jnp.where(first, pltpu.roll(x, w - HEAD_DIM // 2, 1), pltpu.roll(x, HEAD_DIM // 2, 1))
    return x * cos + sw * sin


def _inproj_kernel(x_ref, g_ref, wm_ref, wih_ref, wil_ref, cos_ref, sin_ref,
                   q_ref, kf_ref, kb_ref, vf_ref, vt_ref, u_ref, kif_ref, ki4_ref, qi4_ref, misc_ref):
    xn = _rms(x_ref[...], g_ref[...])
    xh, xl = _split_bf16(xn)
    cos = cos_ref[...]
    sin = sin_ref[...]
    main = jnp.dot(xh, wm_ref[...], preferred_element_type=F32)
    wih = wih_ref[...]
    idx = (jnp.dot(xh, wih, preferred_element_type=F32) + jnp.dot(xl, wih, preferred_element_type=F32)
           + jnp.dot(xh, wil_ref[...], preferred_element_type=F32))

    scale = HEAD_DIM ** -0.5 * LOG2_E
    for h in range(N_HEADS):
        blk = main[:, h * LANES:(h + 1) * LANES]
        q_ref[h * LANES:(h + 1) * LANES, :] = (_rope(blk, cos, sin) * scale).T.astype(BF16)
    k = _rope(main[:, QPAD:QPAD + KV_WIDTH], cos, sin)
    kf_ref[...] = k.T
    kb_ref[...] = k.astype(BF16)
    vt = main[:, QPAD + KV_WIDTH:QPAD + 2 * KV_WIDTH].T
    vf_ref[...] = vt
    vt_ref[:KV_WIDTH, :] = vt.astype(BF16)
    vt_ref[KV_WIDTH:, :] = jnp.ones((VT_ROWS - KV_WIDTH, vt.shape[1]), BF16)
    u_ref[...] = main[:, QPAD + 2 * KV_WIDTH:]

    for h in range(N_IDX_HEADS):
        d = _rope(idx[:, h * LANES:(h + 1) * LANES], cos, sin).T
        hi, lo = _split_bf16(d)
        qi4_ref[2 * h * LANES:(2 * h + 1) * LANES, :] = hi
        qi4_ref[(2 * h + 1) * LANES:(2 * h + 2) * LANES, :] = lo
    off = N_IDX_HEADS * LANES
    d = _rope(idx[:, off:off + LANES], cos, sin)
    kif_ref[...] = d.T[:IDX_DIM, :]
    hi = d.astype(BF16).astype(F32)
    lane = lax.broadcasted_iota(I32, d.shape, 1)
    hl = jnp.where(lane < IDX_DIM, hi, d - hi).astype(BF16)
    ki4_ref[:, :LANES] = hl
    ki4_ref[:, LANES:] = hl
    misc_ref[...] = idx[:, off + LANES:]


def _inproj(x, pos, g_pre, wm, wih, wil, tm):
    B, T, _ = x.shape
    half = HEAD_DIM // 2
    inv = ROPE_THETA ** (-jnp.arange(half, dtype=F32) / half)
    ang = pos.astype(F32)[:, None] * inv[None, :]
    cos = jnp.tile(jnp.cos(ang), (1, LANES // half))
    s = jnp.sin(ang)
    sin = jnp.tile(jnp.concatenate([-s, s], axis=1), (1, LANES // HEAD_DIM))

    def row(w):
        return pl.BlockSpec((None, tm, w), lambda b, t: (b, t, 0))

    def tab():
        return pl.BlockSpec((tm, LANES), lambda b, t: (t, 0))

    def col(w):
        return pl.BlockSpec((None, w, tm), lambda b, t: (b, 0, t))

    out_shapes = (
        jax.ShapeDtypeStruct((B, QPAD, T), BF16),
        jax.ShapeDtypeStruct((B, KV_WIDTH, T), F32),
        jax.ShapeDtypeStruct((B, T, KV_WIDTH), BF16),
        jax.ShapeDtypeStruct((B, KV_WIDTH, T), F32),
        jax.ShapeDtypeStruct((B, VT_ROWS, T), BF16),
        jax.ShapeDtypeStruct((T, B * SSM_WIDTH), F32),
        jax.ShapeDtypeStruct((B, IDX_DIM, T), F32),
        jax.ShapeDtypeStruct((B, T, 4 * IDX_DIM), BF16),
        jax.ShapeDtypeStruct((B, 4 * IDX_DIM * N_IDX_HEADS, T), BF16),
        jax.ShapeDtypeStruct((B, T, LANES), F32),
    )
    out_specs = (
        col(QPAD), col(KV_WIDTH), row(KV_WIDTH), col(KV_WIDTH), col(VT_ROWS),
        pl.BlockSpec((tm, SSM_WIDTH), lambda b, t: (t, b)),
        col(IDX_DIM), row(4 * IDX_DIM), col(4 * IDX_DIM * N_IDX_HEADS), row(LANES),
    )
    return pl.pallas_call(
        _inproj_kernel,
        grid=(B, T // tm),
        in_specs=[row(D_MODEL), _const_spec((1, D_MODEL)), _const_spec((D_MODEL, MAIN_W)),
                  _const_spec((D_MODEL, IDX_W)), _const_spec((D_MODEL, IDX_W)), tab(), tab()],
        out_specs=out_specs,
        out_shape=out_shapes,
        compiler_params=_cparams(2),
        name="inproj",
    )(x, g_pre.reshape(1, D_MODEL), wm, wih, wil, cos, sin)


def _inproj_weights(w_in):
    cols = []
    zero = jnp.zeros((D_MODEL, HEAD_DIM), F32)
    for h in range(N_HEADS):
        wq = w_in[:, OFF_Q + h * HEAD_DIM:OFF_Q + (h + 1) * HEAD_DIM]
        cols += [wq, zero] if h // Q_PER_KV == 0 else [zero, wq]
    cols += [w_in[:, OFF_K:OFF_QI], w_in[:, OFF_U:]]
    wm = jnp.concatenate(cols, axis=1).astype(BF16)
    icols = []
    for h in range(N_IDX_HEADS):
        wq = w_in[:, OFF_QI + h * IDX_DIM:OFF_QI + (h + 1) * IDX_DIM]
        icols += [wq, wq]
    wk = w_in[:, OFF_KI:OFF_WI]
    icols += [wk, wk, w_in[:, OFF_WI:OFF_U], jnp.zeros((D_MODEL, LANES - N_IDX_HEADS), F32)]
    wi = jnp.concatenate(icols, axis=1)
    wih = wi.astype(BF16)
    wil = (wi - wih.astype(F32)).astype(BF16)
    return wm, wih, wil


def _attn_kernel(qi4_ref, wit_ref, ki4_ref, q_ref, kb_ref, vt_ref, o_ref,
                 keys_ref, khi_ref, klo_ref, qs_ref, lg_ref, p_ref, alpha_ref, tie_ref,
                 m_ref, acc_ref, *, topk, tq):
    i = pl.program_id(1)
    nk = i + 1
    w = wit_ref[...]

    def chunk(c):
        return pl.ds(pl.multiple_of(c * tq, tq), tq)

    def score_chunk(c, diagonal):
        kc = ki4_ref[chunk(c), :]
        sc = None
        for h in range(N_IDX_HEADS):
            s = jnp.dot(kc, qi4_ref[4 * IDX_DIM * h:4 * IDX_DIM * (h + 1), :], preferred_element_type=F32)
            t = jnp.maximum(s, 0.0) * w[h:h + 1, :]
            sc = t if sc is None else sc + t
        key = _sortable(sc)
        if diagonal:
            kpos = lax.broadcasted_iota(I32, key.shape, 0)
            qpos = lax.broadcasted_iota(I32, key.shape, 1)
            key = jnp.where(kpos <= qpos, key, INT_MIN)
        keys_ref[chunk(c), :] = key
        khi_ref[chunk(c), :] = (key >> 16).astype(I16)

    def score_pair(cc, _):
        score_chunk(2 * cc, False)
        score_chunk(2 * cc + 1, False)
        return 0

    lax.fori_loop(0, i // 2, score_pair, 0)

    @pl.when(i % 2 == 1)
    def _():
        score_chunk(i - 1, False)

    score_chunk(i, True)

    pack_rows = 2 * SUBLANES
    n_acc = 4

    def count16(ref, cand):
        cand = cand.astype(I16)

        def add_rows(rows, accs):
            m = jnp.where(ref[rows, :] >= cand, jnp.ones((), I16), jnp.zeros((), I16))
            m = m.reshape(m.shape[0] // pack_rows, pack_rows, tq)
            accs = list(accs)
            for r in range(m.shape[0]):
                accs[r % n_acc] = accs[r % n_acc] + m[r]
            return tuple(accs)

        accs = tuple(jnp.zeros((pack_rows, tq), I16) for _ in range(n_acc))
        accs = lax.fori_loop(0, nk // 2, lambda c, a: add_rows(pl.ds(pl.multiple_of(c * 2 * tq, 2 * tq), 2 * tq), a),
                             accs)
        accs = lax.fori_loop(2 * (nk // 2), nk, lambda c, a: add_rows(chunk(c), a), accs)
        tot = (accs[0] + accs[1]) + (accs[2] + accs[3])
        return jnp.sum(tot.astype(I32), axis=0, keepdims=True)

    def search16(ref):
        def bit_body(b, t):
            cand = t + (jnp.int32(1) << (15 - b))
            return jnp.where(count16(ref, cand) >= topk, cand, t)
        return lax.fori_loop(0, 16, bit_body, jnp.full((1, tq), I16_MIN, I32))

    tau_hi = search16(khi_ref)

    def lo_body(c, _):
        k = keys_ref[chunk(c), :]
        hi = k >> 16
        lo = (k & 0xFFFF) + I16_MIN
        lo = jnp.where(hi == tau_hi, lo, jnp.where(hi > tau_hi, I16_MAX, I16_MIN))
        klo_ref[chunk(c), :] = lo.astype(I16)
        return 0

    lax.fori_loop(0, nk, lo_body, 0)
    tau_lo = search16(klo_ref)
    tau = (tau_hi << 16) + (tau_lo - I16_MIN)
    tau = jnp.maximum(tau, INT_MIN + 1)

    def gt_body(c, acc):
        m = jnp.where(keys_ref[chunk(c), :] > tau, 1, 0).astype(I32)
        return acc + jnp.sum(m.reshape(tq // SUBLANES, SUBLANES, tq), axis=0)

    n_gt = jnp.sum(lax.fori_loop(0, nk, gt_body, jnp.zeros((SUBLANES, tq), I32)), axis=0, keepdims=True)
    quota = (topk - n_gt).astype(F32)

    for h in range(N_HEADS):
        qs_ref[:, h * tq:(h + 1) * tq] = q_ref[h * LANES:(h + 1) * LANES, :]
    m_ref[...] = jnp.full(m_ref.shape, NEG, F32)
    acc_ref[...] = jnp.zeros(acc_ref.shape, F32)
    r_i = lax.broadcasted_iota(I32, (tq, tq), 0)
    c_i = lax.broadcasted_iota(I32, (tq, tq), 1)
    ltri = jnp.where(c_i <= r_i, 1.0, 0.0).astype(BF16)

    def prepare(c, slot):
        cc = jnp.minimum(c, nk - 1)
        kc = keys_ref[chunk(cc), :]
        eq = kc == tau
        e = jnp.where(eq, 1.0, 0.0)
        pre = jnp.dot(ltri, e.astype(BF16), preferred_element_type=F32)
        rank = tie_ref[...] + pre - e
        sel = ((kc > tau) | (eq & (rank < quota))) & (c < nk)
        tie_ref[...] = tie_ref[...] + pre[tq - 1:tq, :]
        bias = jnp.where(sel, 0.0, NEG)
        lg = jnp.dot(kb_ref[chunk(cc), :], qs_ref[...], preferred_element_type=F32)
        for h in range(N_HEADS):
            lg_ref[slot, :, h * tq:(h + 1) * tq] = lg[:, h * tq:(h + 1) * tq] + bias

    def softmax(slot):
        for j in range(N_HEADS * tq // LANES):
            cols = slice(j * LANES, (j + 1) * LANES)
            x = lg_ref[slot, :, cols]
            m_prev = m_ref[:, cols]
            m_new = jnp.maximum(m_prev, jnp.max(x, axis=0, keepdims=True))
            alpha = jnp.exp2(m_prev - m_new)
            m_ref[:, cols] = m_new
            p_ref[slot, :, cols] = jnp.exp2(x - m_new).astype(BF16)
            alpha_ref[slot, :, cols] = alpha

    def value(c, slot):
        cc = jnp.clip(c, 0, nk - 1)
        acc_ref[...] = alpha_ref[slot] * acc_ref[...] + jnp.dot(vt_ref[:, chunk(cc)], p_ref[slot],
                                                                preferred_element_type=F32)

    tie_ref[...] = jnp.zeros(tie_ref.shape, F32)
    p_ref[1] = jnp.zeros(p_ref.shape[1:], BF16)
    alpha_ref[1] = jnp.ones(alpha_ref.shape[1:], F32)
    prepare(0, 0)

    def pair_body(cc, _):
        for slot in range(2):
            c = 2 * cc + slot
            prepare(c + 1, 1 - slot)
            softmax(slot)
            value(c - 1, 1 - slot)
        return 0

    n_pairs = (nk + 1) // 2
    lax.fori_loop(0, n_pairs, pair_body, 0)
    value(2 * n_pairs - 1, 1)
    out = acc_ref[:KV_WIDTH, :] / acc_ref[KV_WIDTH:KV_WIDTH + 1, :]
    for h in range(N_HEADS):
        o_ref[:, h * LANES:(h + 1) * LANES] = out[:, h * tq:(h + 1) * tq].T.astype(BF16)


def _prompt_attention(qi4, wit, ki4, q, kb, vt, tq):
    B, T, _ = kb.shape
    topk = min(TOPK_MAX, T // 4)

    def tile(w):
        return pl.BlockSpec((None, tq, w), lambda b, i: (b, i, 0))

    def full(w):
        return pl.BlockSpec((None, T, w), lambda b, i: (b, 0, 0))

    return pl.pallas_call(
        functools.partial(_attn_kernel, topk=topk, tq=tq),
        grid=(B, T // tq),
        in_specs=[pl.BlockSpec((None, 4 * IDX_DIM * N_IDX_HEADS, tq), lambda b, i: (b, 0, i)),
                  pl.BlockSpec((None, N_IDX_HEADS, tq), lambda b, i: (b, 0, i)),
                  full(4 * IDX_DIM), pl.BlockSpec((None, QPAD, tq), lambda b, i: (b, 0, i)), full(KV_WIDTH),
                  pl.BlockSpec((None, VT_ROWS, T), lambda b, i: (b, 0, 0))],
        out_specs=tile(QPAD),
        out_shape=jax.ShapeDtypeStruct((B, T, QPAD), BF16),
        scratch_shapes=[pltpu.VMEM((T, tq), I32), pltpu.VMEM((T, tq), I16), pltpu.VMEM((T, tq), I16),
                        pltpu.VMEM((LANES, N_HEADS * tq), BF16), pltpu.VMEM((2, tq, N_HEADS * tq), F32),
                        pltpu.VMEM((2, tq, N_HEADS * tq), BF16),
                        pltpu.VMEM((2, 1, N_HEADS * tq), F32), pltpu.VMEM((1, tq), F32),
                        pltpu.VMEM((1, N_HEADS * tq), F32), pltpu.VMEM((VT_ROWS, N_HEADS * tq), F32)],
        compiler_params=_cparams(2),
        name="prompt_attention",
    )(qi4, wit, ki4, q, kb, vt)


def _s5_prep_kernel(lr_ref, li_ref, ldt_ref, brt_ref, bit_ref, ar_ref, ai_ref, bbr_ref, bbi_ref):
    lr = lr_ref[...]
    li = li_ref[...]
    dt = jnp.exp(ldt_ref[...])
    mag = jnp.exp(lr * dt)
    ar = mag * jnp.cos(li * dt)
    ai = mag * jnp.sin(li * dt)
    den = lr * lr + li * li
    cr = ((ar - 1.0) * lr + ai * li) / den
    ci = (ai * lr - (ar - 1.0) * li) / den
    ar_ref[...] = ar
    ai_ref[...] = ai
    brt = brt_ref[...]
    bit = bit_ref[...]
    bbr_ref[...] = cr[:, None, :] * brt - ci[:, None, :] * bit
    bbi_ref[...] = cr[:, None, :] * bit + ci[:, None, :] * brt


def _s5_weights(lambda_re, lambda_im, log_dt, b_re, b_im, c_re, c_im):
    G, P, C = N_SSM_GROUPS, SSM_STATE, SSM_GROUP
    ar, ai, bbr, bbi = pl.pallas_call(
        _s5_prep_kernel,
        out_shape=(jax.ShapeDtypeStruct((G, P), F32), jax.ShapeDtypeStruct((G, P), F32),
                   jax.ShapeDtypeStruct((G, C, P), F32), jax.ShapeDtypeStruct((G, C, P), F32)),
        name="s5_prep",
    )(lambda_re, lambda_im, log_dt.reshape(G, 1), jnp.swapaxes(b_re, 1, 2), jnp.swapaxes(b_im, 1, 2))
    eye = jnp.eye(GROUPS_PER_CHUNK, dtype=F32)

    def in_blocks(bb):
        x = bb.reshape(N_UCHUNK, GROUPS_PER_CHUNK, C, P)
        x = x[:, :, :, None, :] * eye[None, :, None, :, None]
        return x.reshape(N_UCHUNK, LANES, GROUPS_PER_CHUNK * P)

    def out_blocks(cc):
        x = jnp.swapaxes(cc, 1, 2).reshape(N_UCHUNK, GROUPS_PER_CHUNK, P, C)
        x = x[:, :, :, None, :] * eye[None, :, None, :, None]
        return x.reshape(N_UCHUNK, GROUPS_PER_CHUNK * P, LANES)

    bblk = jnp.concatenate([in_blocks(bbr), in_blocks(bbi)], axis=2).astype(BF16)
    cblk = jnp.concatenate([out_blocks(c_re), out_blocks(-c_im)], axis=1).astype(BF16)
    half = SLABS_PER_CHUNK // 2

    def slabs(a):
        x = a.reshape(N_UCHUNK, 1, half, 1, LANES)
        return jnp.broadcast_to(x, (N_UCHUNK, 2, half, 1, LANES)).reshape(N_SLAB, 1, LANES)

    return slabs(ar), slabs(ai), bblk, cblk


def _state_to_slabs(h_re, h_im):
    B = h_re.shape[0]
    half = SLABS_PER_CHUNK // 2
    re = h_re.reshape(B, N_UCHUNK, 1, half, LANES)
    im = h_im.reshape(B, N_UCHUNK, 1, half, LANES)
    x = jnp.concatenate([re, im], axis=2).reshape(B, N_SLAB, LANES)
    return jnp.swapaxes(x, 0, 1)


def _slabs_to_state(h):
    B = h.shape[1]
    half = SLABS_PER_CHUNK // 2
    x = jnp.swapaxes(h, 0, 1).reshape(B, N_UCHUNK, 2, half * LANES)
    re = x[:, :, 0].reshape(B, N_SSM_GROUPS, SSM_STATE)
    im = x[:, :, 1].reshape(B, N_SSM_GROUPS, SSM_STATE)
    return re, im


def _s5_kernel(u_ref, h0_ref, ar_ref, ai_ref, bblk_ref, cblk_ref, d_ref, y_ref, hout_ref, bu_ref, hs_ref,
               *, nb, tj, lane_batches):
    rows = tj * nb
    half = SLABS_PER_CHUNK // 2
    blk_rows = rows // lane_batches

    @pl.when(pl.program_id(0) == 0)
    def _():
        hout_ref[...] = h0_ref[...]

    def batch_rows(lb):
        if lane_batches == 1:
            return pl.ds(0, blk_rows)
        return pl.ds(lb, blk_rows, stride=lane_batches)

    def u_cols(lb, q):
        return slice(lb * SSM_WIDTH + q * LANES, lb * SSM_WIDTH + (q + 1) * LANES)

    for q in range(N_UCHUNK):
        ub = jnp.concatenate([u_ref[:, u_cols(lb, q)] for lb in range(lane_batches)], axis=0).astype(BF16)
        bu = jnp.dot(ub, bblk_ref[q], preferred_element_type=F32)
        for lb in range(lane_batches):
            for s in range(SLABS_PER_CHUNK):
                bu_ref[q * SLABS_PER_CHUNK + s, batch_rows(lb), :] = bu[lb * blk_rows:(lb + 1) * blk_rows,
                                                                        s * LANES:(s + 1) * LANES]

    def step(j, h):
        cur = pl.ds(j * nb, nb)
        new = list(h)
        for q in range(N_UCHUNK):
            for r in range(half):
                sr = q * SLABS_PER_CHUNK + r
                si = sr + half
                a_re = ar_ref[sr]
                a_im = ai_ref[sr]
                new[sr] = a_re * h[sr] - a_im * h[si] + bu_ref[sr, cur, :]
                new[si] = a_re * h[si] + a_im * h[sr] + bu_ref[si, cur, :]
                hs_ref[sr, cur, :] = new[sr]
                hs_ref[si, cur, :] = new[si]
        return tuple(new)

    h = lax.fori_loop(0, tj, step, tuple(hout_ref[s] for s in range(N_SLAB)))
    for s in range(N_SLAB):
        hout_ref[s] = h[s]

    for q in range(N_UCHUNK):
        hq = jnp.concatenate(
            [jnp.concatenate([hs_ref[q * SLABS_PER_CHUNK + s, batch_rows(lb), :] for s in range(SLABS_PER_CHUNK)],
                             axis=1) for lb in range(lane_batches)], axis=0).astype(BF16)
        yq = jnp.dot(hq, cblk_ref[q], preferred_element_type=F32)
        for lb in range(lane_batches):
            cols = u_cols(lb, q)
            y_ref[:, cols] = (yq[lb * blk_rows:(lb + 1) * blk_rows]
                              + d_ref[:, q * LANES:(q + 1) * LANES] * u_ref[:, cols])


def _s5_scan(u, nb, lane_batches, h0_slabs, weights, d_skip, tj):
    T = u.shape[0] * lane_batches // nb
    ar, ai, bblk, cblk = weights
    rows = tj * nb
    blk = (rows // lane_batches, lane_batches * SSM_WIDTH)
    y, hout = pl.pallas_call(
        functools.partial(_s5_kernel, nb=nb, tj=tj, lane_batches=lane_batches),
        grid=(T // tj,),
        in_specs=[pl.BlockSpec(blk, lambda t: (t, 0)),
                  _const_spec((N_SLAB, nb, LANES)), _const_spec((N_SLAB, 1, LANES)),
                  _const_spec((N_SLAB, 1, LANES)),
                  _const_spec((N_UCHUNK, LANES, SLABS_PER_CHUNK * LANES)),
                  _const_spec((N_UCHUNK, SLABS_PER_CHUNK * LANES, LANES)),
                  _const_spec((1, SSM_WIDTH))],
        out_specs=(pl.BlockSpec(blk, lambda t: (t, 0)),
                   pl.BlockSpec((N_SLAB, nb, LANES), lambda t: (0, 0, 0))),
        out_shape=(jax.ShapeDtypeStruct(u.shape, F32),
                   jax.ShapeDtypeStruct((N_SLAB, nb, LANES), F32)),
        scratch_shapes=[pltpu.VMEM((N_SLAB, rows, LANES), F32), pltpu.VMEM((N_SLAB, rows, LANES), F32)],
        compiler_params=_cparams(1),
        name="s5_scan",
    )(u, h0_slabs, ar, ai, bblk, cblk, d_skip.reshape(1, SSM_WIDTH))
    return y, hout


def _post_kernel(x_ref, ao_ref, y_ref, wglu_ref, bglu_ref, woa_ref, wos_ref, gpost_ref, gpre_ref,
                 wup_ref, wdn_ref, gfpost_ref, o_ref):
    z = jax.nn.gelu(y_ref[...])
    gate = jax.nn.sigmoid(jnp.dot(z.astype(BF16), wglu_ref[...], preferred_element_type=F32) + bglu_ref[...])
    so = z * gate
    mix = (jnp.dot(ao_ref[...], woa_ref[...], preferred_element_type=F32)
           + jnp.dot(so.astype(BF16), wos_ref[...], preferred_element_type=F32))
    x = x_ref[...] + _rms(mix, gpost_ref[...])
    hdn = jnp.dot(_rms(x, gpre_ref[...]).astype(BF16), wup_ref[...], preferred_element_type=F32)
    hdn = jnp.square(jnp.maximum(hdn, 0.0))
    dn = jnp.dot(hdn.astype(BF16), wdn_ref[...], preferred_element_type=F32)
    o_ref[...] = x + _rms(dn, gfpost_ref[...])


def _post_weights(w_glu, w_out, w_up, w_down):
    zero = jnp.zeros((HEAD_DIM, D_MODEL), F32)
    rows = []
    for h in range(N_HEADS):
        wh = w_out[h * HEAD_DIM:(h + 1) * HEAD_DIM]
        rows += [wh, zero] if h // Q_PER_KV == 0 else [zero, wh]
    woa = jnp.concatenate(rows, axis=0).astype(BF16)
    return (w_glu.astype(BF16), woa, w_out[ATTN_WIDTH:].astype(BF16), w_up.astype(BF16), w_down.astype(BF16))


def _post(x, ao, y_tb, weights, b_glu, g_post, g_ffn_pre, g_ffn_post, tm):
    B, T, _ = x.shape
    wglu, woa, wos, wup, wdn = weights

    def row(w):
        return pl.BlockSpec((None, tm, w), lambda b, t: (b, t, 0))

    def vec(a):
        return a.reshape(1, -1)

    return pl.pallas_call(
        _post_kernel,
        grid=(B, T // tm),
        in_specs=[row(D_MODEL), row(QPAD), pl.BlockSpec((tm, SSM_WIDTH), lambda b, t: (t, b)),
                  _const_spec((SSM_WIDTH, SSM_WIDTH)), _const_spec((1, SSM_WIDTH)),
                  _const_spec((QPAD, D_MODEL)), _const_spec((SSM_WIDTH, D_MODEL)),
                  _const_spec((1, D_MODEL)), _const_spec((1, D_MODEL)),
                  _const_spec((D_MODEL, D_FF)), _const_spec((D_FF, D_MODEL)), _const_spec((1, D_MODEL))],
        out_specs=row(D_MODEL),
        out_shape=jax.ShapeDtypeStruct((B, T, D_MODEL), F32),
        compiler_params=_cparams(2),
        name="post",
    )(x, ao, y_tb, wglu, vec(b_glu), woa, wos, vec(g_post), vec(g_ffn_pre), wup, wdn, vec(g_ffn_post))


def _page_map(b, j, pt, *, r, n):
    return (pt[b, j * n + r], 0, 0)


def _skeys_kernel(pt_ref, qh_ref, ql_ref, w_ref, kn_ref, *rest, n_pages_step, dec_t):
    page_refs = rest[:n_pages_step]
    keys_ref, knew_ref = rest[n_pages_step:]
    j = pl.program_id(1)
    qh = qh_ref[...]
    ql = ql_ref[...]
    w = w_ref[...]

    def scores(kc):
        kh, kl = _split_bf16(kc)
        s = (jnp.dot(qh, kh, preferred_element_type=F32) + jnp.dot(ql, kh, preferred_element_type=F32)
             + jnp.dot(qh, kl, preferred_element_type=F32))
        sc = None
        for h in range(N_IDX_HEADS):
            t = jnp.maximum(s[h * dec_t:(h + 1) * dec_t], 0.0) * w[:, h:h + 1]
            sc = t if sc is None else sc + t
        return _sortable(sc)

    keys_ref[...] = scores(jnp.concatenate([r[...] for r in page_refs], axis=1))

    @pl.when(j == pl.num_programs(1) - 1)
    def _():
        kn = scores(kn_ref[...])
        qi_ = lax.broadcasted_iota(I32, kn.shape, 0)
        ki_ = lax.broadcasted_iota(I32, kn.shape, 1)
        knew_ref[...] = jnp.where(ki_ <= qi_, kn, INT_MIN)


def _ssearch_kernel(kp_ref, kn_ref, sel_ref, *, topk, past_len):
    def count(pred_past, pred_new):
        c = jnp.sum(jnp.where(pred_past, 1, 0).astype(I32), axis=1, keepdims=True)
        return c + jnp.sum(jnp.where(pred_new, 1, 0).astype(I32), axis=1, keepdims=True)

    def count_ge(cand):
        return count(kp_ref[...] >= cand, kn_ref[...] >= cand)

    rows = kp_ref.shape[0]
    tau = jnp.where(count_ge(jnp.zeros((rows, 1), I32)) >= topk, 0, INT_MIN).astype(I32)

    def bit_body(b, tau):
        cand = tau + (jnp.int32(1) << (30 - b))
        return jnp.where(count_ge(cand) >= topk, cand, tau)

    tau = lax.fori_loop(0, 31, bit_body, tau)
    tau = jnp.maximum(tau, INT_MIN + 1)
    quota = topk - count(kp_ref[...] > tau, kn_ref[...] > tau)
    pos_p = lax.broadcasted_iota(I32, kp_ref.shape, 1)
    pos_n = past_len + lax.broadcasted_iota(I32, kn_ref.shape, 1)
    n_bits = max(1, (past_len + kn_ref.shape[1] - 1).bit_length())

    def idx_body(b, lim):
        cand = lim + (jnp.int32(1) << (n_bits - 1 - b))
        ties = count((kp_ref[...] == tau) & (pos_p < cand), (kn_ref[...] == tau) & (pos_n < cand))
        return jnp.where(ties < quota, cand, lim)

    lim = lax.fori_loop(0, n_bits, idx_body, jnp.zeros_like(tau))
    kp = kp_ref[...]
    kn = kn_ref[...]
    sel_ref[:, :past_len] = jnp.where((kp > tau) | ((kp == tau) & (pos_p <= lim)), 1.0, 0.0)
    sel_ref[:, past_len:] = jnp.where((kn > tau) | ((kn == tau) & (pos_n <= lim)), 1.0, 0.0)


def _sample_select(page_table, qh, ql, w, ki_new_pad, cache_kidx, n_pages_step):
    DB, n_pages = page_table.shape
    dec_t = w.shape[1]
    past_len = n_pages * PAGE_SIZE
    lp = past_len + LANES
    topk = min(TOPK_MAX, (past_len + dec_t) // 4)
    rows = N_IDX_HEADS * dec_t
    step_keys = n_pages_step * PAGE_SIZE

    def fixed(shape):
        return pl.BlockSpec((None,) + shape, lambda b, j, pt: (b, 0, 0))

    pages = [pl.BlockSpec((None, IDX_DIM, PAGE_SIZE), functools.partial(_page_map, r=r, n=n_pages_step))
             for r in range(n_pages_step)]
    grid_spec = pltpu.PrefetchScalarGridSpec(
        num_scalar_prefetch=1,
        grid=(DB, n_pages // n_pages_step),
        in_specs=[fixed((rows, IDX_DIM)), fixed((rows, IDX_DIM)), fixed((dec_t, N_IDX_HEADS)),
                  fixed((IDX_DIM, LANES))] + pages,
        out_specs=(pl.BlockSpec((None, dec_t, step_keys), lambda b, j, pt: (b, 0, j)), fixed((dec_t, LANES))),
    )
    keys_past, keys_new = pl.pallas_call(
        functools.partial(_skeys_kernel, n_pages_step=n_pages_step, dec_t=dec_t),
        grid_spec=grid_spec,
        out_shape=(jax.ShapeDtypeStruct((DB, dec_t, past_len), I32), jax.ShapeDtypeStruct((DB, dec_t, LANES), I32)),
        compiler_params=_cparams(2),
        name="sample_keys",
    )(page_table, qh, ql, w, ki_new_pad, *([cache_kidx] * n_pages_step))
    sel = pl.pallas_call(
        functools.partial(_ssearch_kernel, topk=topk, past_len=past_len),
        out_shape=jax.ShapeDtypeStruct((DB * dec_t, lp), F32),
        compiler_params=pltpu.CompilerParams(vmem_limit_bytes=VMEM_LIMIT),
        name="sample_search",
    )(keys_past.reshape(DB * dec_t, past_len), keys_new.reshape(DB * dec_t, LANES))
    return sel.reshape(DB, dec_t, lp)


def _sattn_kernel(pt_ref, sel_ref, q_ref, kn_ref, vn_ref, *rest, n_pages_step, past_len, dec_t):
    k_refs = rest[:n_pages_step]
    v_refs = rest[n_pages_step:2 * n_pages_step]
    o_ref, m_ref, l_ref, acc_ref = rest[2 * n_pages_step:]
    j = pl.program_id(1)
    step_keys = n_pages_step * PAGE_SIZE
    q = q_ref[...]

    @pl.when(j == 0)
    def _():
        m_ref[...] = jnp.full(m_ref.shape, NEG, F32)
        l_ref[...] = jnp.zeros(l_ref.shape, F32)
        acc_ref[...] = jnp.zeros(acc_ref.shape, F32)

    def update(kc, vc, sel):
        n = kc.shape[1]
        lg = jnp.dot(q, kc, preferred_element_type=F32)
        lg = jnp.where(sel[None] > 0.5, lg.reshape(N_HEADS, dec_t, n), NEG).reshape(N_HEADS * dec_t, n)
        m_prev = m_ref[...]
        m_new = jnp.maximum(m_prev, jnp.max(lg, axis=1, keepdims=True))
        alpha = jnp.exp2(m_prev - m_new)
        p = jnp.exp2(lg - m_new)
        l_ref[...] = alpha * l_ref[...] + jnp.sum(p, axis=1, keepdims=True)
        acc_ref[...] = alpha * acc_ref[...] + lax.dot_general(p.astype(BF16), vc, NT_DIMS,
                                                              preferred_element_type=F32)
        m_ref[...] = m_new

    kc = jnp.concatenate([r[...] for r in k_refs], axis=1).astype(BF16)
    vc = jnp.concatenate([r[...] for r in v_refs], axis=1).astype(BF16)
    update(kc, vc, sel_ref[:, pl.ds(pl.multiple_of(j * step_keys, step_keys), step_keys)])

    @pl.when(j == pl.num_programs(1) - 1)
    def _():
        update(kn_ref[...], vn_ref[...], sel_ref[:, past_len:])
        o_ref[...] = acc_ref[...] / l_ref[...]


def _sample_attention(page_table, sel, q, k_new_pad, v_new_pad, cache_k, cache_v, n_pages_step):
    DB, n_pages = page_table.shape
    dec_t = sel.shape[1]
    past_len = n_pages * PAGE_SIZE
    lp = past_len + LANES
    rows = N_HEADS * dec_t

    def fixed(shape):
        return pl.BlockSpec((None,) + shape, lambda b, j, pt: (b, 0, 0))

    def pages():
        return [pl.BlockSpec((None, KV_WIDTH, PAGE_SIZE), functools.partial(_page_map, r=r, n=n_pages_step))
                for r in range(n_pages_step)]

    grid_spec = pltpu.PrefetchScalarGridSpec(
        num_scalar_prefetch=1,
        grid=(DB, n_pages // n_pages_step),
        in_specs=[fixed((dec_t, lp)), fixed((rows, KV_WIDTH)), fixed((KV_WIDTH, LANES)),
                  fixed((KV_WIDTH, LANES))] + pages() + pages(),
        out_specs=fixed((rows, KV_WIDTH)),
        scratch_shapes=[pltpu.VMEM((rows, 1), F32), pltpu.VMEM((rows, 1), F32),
                        pltpu.VMEM((rows, KV_WIDTH), F32)],
    )
    return pl.pallas_call(
        functools.partial(_sattn_kernel, n_pages_step=n_pages_step, past_len=past_len, dec_t=dec_t),
        grid_spec=grid_spec,
        out_shape=jax.ShapeDtypeStruct((DB, rows, KV_WIDTH), F32),
        compiler_params=_cparams(2),
        name="sample_attention",
    )(page_table, sel, q, k_new_pad, v_new_pad, *([cache_k] * n_pages_step), *([cache_v] * n_pages_step))


def _largest_tile(n, cap):
    t = min(n, cap)
    while n % t:
        t //= 2
    return t


def _layer(hp, hs, cache_k, cache_v, cache_kidx, st_re, st_im, page_table, g_pre, w_in, ssm_params, d_skip,
           w_glu, b_glu, w_out, g_post, g_ffn_pre, w_up, w_down, g_ffn_post):
    B, T, _ = hp.shape
    DB, dec_t, _ = hs.shape
    n_pages = page_table.shape[1]
    past_len = n_pages * PAGE_SIZE
    idx_scale = (IDX_DIM ** -0.5) * (N_IDX_HEADS ** -0.5)

    wm, wih, wil = _inproj_weights(w_in)
    s5w = _s5_weights(*ssm_params)
    postw = _post_weights(w_glu, w_out, w_up, w_down)

    pos_p = jnp.arange(T, dtype=I32)
    q, kft, kb, vft, vt, u_tb, kift, ki4, qi4, misc = _inproj(hp, pos_p, g_pre, wm, wih, wil, _largest_tile(T, 512))
    wit = jnp.swapaxes(misc[:, :, :N_IDX_HEADS], 1, 2) * idx_scale
    ao = _prompt_attention(qi4, wit, ki4, q, kb, vt, _largest_tile(T, 256))
    h0 = jnp.zeros((N_SLAB, B, LANES), F32)
    y_tb, hfin = _s5_scan(u_tb, B, B, h0, s5w, d_skip, _largest_tile(T, 256))
    yp = _post(hp, ao, y_tb, postw, b_glu, g_post, g_ffn_pre, g_ffn_post, _largest_tile(T, 512))
    srp, sip = _slabs_to_state(hfin)

    def kv_rows(a, nb, t):
        return jnp.transpose(a.reshape(nb, N_KV_HEADS, HEAD_DIM, t), (0, 3, 1, 2))

    prompt_out = (yp, kv_rows(kft, B, T), kv_rows(vft, B, T), jnp.swapaxes(kift, 1, 2), srp, sip)

    n_s = DB * dec_t
    pos_s = jnp.tile(past_len + jnp.arange(dec_t, dtype=I32), DB)
    q, kft, _, vft, _, u_s, kift, _, qi4, misc = _inproj(hs.reshape(1, n_s, D_MODEL), pos_s, g_pre, wm, wih, wil,
                                                        _largest_tile(n_s, 256))
    qi4 = jnp.transpose(qi4.reshape(N_IDX_HEADS, 4 * IDX_DIM, DB, dec_t), (2, 0, 3, 1))

    def heads_first(a):
        return a.reshape(DB, a.shape[1] * dec_t, a.shape[3])

    def per_batch(a):
        return jnp.swapaxes(a.reshape(a.shape[1], DB, dec_t), 0, 1)

    def pad_new(a, dtype):
        return jnp.pad(a, ((0, 0), (0, 0), (0, LANES - dec_t))).astype(dtype)

    k_s, v_s, ki_s = per_batch(kft), per_batch(vft), per_batch(kift)
    qh = heads_first(qi4[..., :IDX_DIM])
    ql = heads_first(qi4[..., 2 * IDX_DIM:3 * IDX_DIM])
    w_s = misc.reshape(DB, dec_t, LANES)[:, :, :N_IDX_HEADS] * idx_scale

    n_phys = cache_k.shape[0]
    pool_ki = jnp.swapaxes(cache_kidx, 1, 2)
    pool_k = jnp.transpose(cache_k, (0, 2, 3, 1)).reshape(n_phys, KV_WIDTH, PAGE_SIZE)
    pool_v = jnp.transpose(cache_v, (0, 2, 3, 1)).reshape(n_phys, KV_WIDTH, PAGE_SIZE)
    n_step = _largest_tile(n_pages, 32)
    sel = _sample_select(page_table, qh, ql, w_s, pad_new(ki_s, F32), pool_ki, n_step)
    qs = jnp.transpose(q.reshape(N_HEADS, LANES, DB, dec_t), (2, 0, 3, 1)).reshape(DB, N_HEADS * dec_t, LANES)
    ao_s = _sample_attention(page_table, sel, qs, pad_new(k_s, BF16), pad_new(v_s, BF16), pool_k, pool_v, n_step)
    ao_s = jnp.swapaxes(ao_s.reshape(DB, N_HEADS, dec_t, LANES), 1, 2).reshape(1, n_s, QPAD).astype(BF16)
    u_t = jnp.swapaxes(u_s.reshape(DB, dec_t, SSM_WIDTH), 0, 1).reshape(n_s, SSM_WIDTH)
    y_t, hfin = _s5_scan(u_t, DB, 1, _state_to_slabs(st_re, st_im), s5w, d_skip, dec_t)
    y_s = jnp.swapaxes(y_t.reshape(dec_t, DB, SSM_WIDTH), 0, 1).reshape(n_s, SSM_WIDTH)
    ys = _post(hs.reshape(1, n_s, D_MODEL), ao_s, y_s, postw, b_glu, g_post, g_ffn_pre, g_ffn_post,
               _largest_tile(n_s, 256))
    srs, sis = _slabs_to_state(hfin)
    sample_out = (ys.reshape(DB, dec_t, D_MODEL), kv_rows(k_s, DB, dec_t), kv_rows(v_s, DB, dec_t),
                  jnp.swapaxes(ki_s, 1, 2), srs, sis)
    return prompt_out, sample_out


def kernel(x_prompt, x_sample, cache_k, cache_v, cache_kidx, state_ssm_re, state_ssm_im, page_table, norm_mix_pre, w_in, lambda_re, lambda_im, log_dt, b_re, b_im, c_re, c_im, d_skip, w_glu, b_glu, w_out, norm_mix_post, norm_ffn_pre, w_up, w_down, norm_ffn_post):
    depth = w_in.shape[0]
    hp, hs = x_prompt, x_sample
    p_acc = [[] for _ in range(5)]
    s_acc = [[] for _ in range(5)]
    for l in range(depth):
        ssm_params = (lambda_re[l], lambda_im[l], log_dt[l], b_re[l], b_im[l], c_re[l], c_im[l])
        p_out, s_out = _layer(hp, hs, cache_k[l], cache_v[l], cache_kidx[l], state_ssm_re[l], state_ssm_im[l],
                              page_table, norm_mix_pre[l], w_in[l], ssm_params, d_skip[l], w_glu[l], b_glu[l],
                              w_out[l], norm_mix_post[l], norm_ffn_pre[l], w_up[l], w_down[l], norm_ffn_post[l])
        hp, hs = p_out[0], s_out[0]
        for acc, vals in ((p_acc, p_out[1:]), (s_acc, s_out[1:])):
            for a, v in zip(acc, vals):
                a.append(v)
    return (hp, hs) + tuple(jnp.stack(a) for a in p_acc) + tuple(jnp.stack(a) for a in s_acc)
```

```python
import functools
import math

import jax
import jax.numpy as jnp
from jax import lax
from jax.experimental import pallas as pl
from jax.experimental.pallas import tpu as pltpu

F32 = jnp.float32
BF16 = jnp.bfloat16
I32 = jnp.int32

D_MODEL = 1024
PAGE_SIZE = 128
HEAD_DIM = 64
ATTN_WIDTH = D_MODEL // 2
N_HEADS = ATTN_WIDTH // HEAD_DIM
N_KV_HEADS = 2
Q_PER_KV = N_HEADS // N_KV_HEADS
KV_WIDTH = N_KV_HEADS * HEAD_DIM
N_IDX_HEADS = 4
IDX_DIM = 64
TOPK_MAX = 256
ROPE_THETA = 10000.0
SSM_WIDTH = D_MODEL - ATTN_WIDTH
SSM_GROUP = 16
N_SSM_GROUPS = SSM_WIDTH // SSM_GROUP
SSM_STATE = 64
D_FF = 4 * D_MODEL
RMS_EPS = 1e-6
NEG = -1e30
OFF_Q = 0
OFF_K = OFF_Q + ATTN_WIDTH
OFF_V = OFF_K + KV_WIDTH
OFF_QI = OFF_V + KV_WIDTH
OFF_KI = OFF_QI + N_IDX_HEADS * IDX_DIM
OFF_WI = OFF_KI + IDX_DIM
OFF_U = OFF_WI + N_IDX_HEADS
IN_WIDTH = OFF_U + SSM_WIDTH

LANES = 128
SUBLANES = 8
INT_MIN = -(2 ** 31)
VMEM_LIMIT = 56 * 1024 * 1024

QPAD = N_HEADS * LANES
MAIN_W = QPAD + 2 * KV_WIDTH + SSM_WIDTH
IDX_W = N_IDX_HEADS * IDX_DIM + LANES
SSM_LANES = N_SSM_GROUPS * SSM_STATE
N_SLAB = 2 * SSM_LANES // LANES
N_UCHUNK = SSM_WIDTH // LANES
GROUPS_PER_CHUNK = LANES // SSM_GROUP
SLABS_PER_CHUNK = N_SLAB // N_UCHUNK

NT_DIMS = (((1,), (1,)), ((), ()))
LOG2_E = math.log2(math.e)
I16 = jnp.int16
I16_MIN = -(2 ** 15)
I16_MAX = 2 ** 15 - 1
VG_ROWS = HEAD_DIM + 2 * SUBLANES
VT_ROWS = N_KV_HEADS * VG_ROWS


def _cparams(n_grid):
    return pltpu.CompilerParams(dimension_semantics=("arbitrary",) * n_grid,
                                vmem_limit_bytes=VMEM_LIMIT)


def _const_spec(shape):
    zeros = (0,) * len(shape)
    return pl.BlockSpec(shape, lambda *_: zeros, pipeline_mode=pl.Buffered(1))


def _rms(x, g):
    return x * lax.rsqrt(jnp.mean(x * x, axis=-1, keepdims=True) + RMS_EPS) * g


def _split_bf16(x):
    hi = x.astype(BF16)
    lo = (x - hi.astype(F32)).astype(BF16)
    return hi, lo


def _sortable(x):
    b = lax.bitcast_convert_type(x, I32)
    return b ^ ((b >> 31) & jnp.int32(0x7FFFFFFF))


def _rope(x, cos, sin):
    w = x.shape[-1]
    lane = lax.broadcasted_iota(I32, x.shape, 1)
    first = (lane & (HEAD_DIM - 1)) < HEAD_DIM // 2
    sw = jnp.where(first, pltpu.roll(x, w - HEAD_DIM // 2, 1), pltpu.roll(x, HEAD_DIM // 2, 1))
    return x * cos + sw * sin


def _inproj_kernel(x_ref, g_ref, wm_ref, wih_ref, wil_ref, cos_ref, sin_ref,
                   q_ref, kf_ref, kb_ref, vf_ref, vt_ref, u_ref, kif_ref, ki4_ref, qi4_ref, misc_ref):
    xn = _rms(x_ref[...], g_ref[...])
    xh, xl = _split_bf16(xn)
    cos = cos_ref[...]
    sin = sin_ref[...]
    main = jnp.dot(xh, wm_ref[...], preferred_element_type=F32)
    wih = wih_ref[...]
    idx = (jnp.dot(xh, wih, preferred_element_type=F32) + jnp.dot(xl, wih, preferred_element_type=F32)
           + jnp.dot(xh, wil_ref[...], preferred_element_type=F32))

    scale = HEAD_DIM ** -0.5 * LOG2_E
    for h in range(N_HEADS):
        blk = main[:, h * LANES:(h + 1) * LANES]
        q_ref[h * LANES:(h + 1) * LANES, :] = (_rope(blk, cos, sin) * scale).T.astype(BF16)
    k = _rope(main[:, QPAD:QPAD + KV_WIDTH], cos, sin)
    kf_ref[...] = k.T
    kb_ref[...] = k.astype(BF16)
    vt = main[:, QPAD + KV_WIDTH:QPAD + 2 * KV_WIDTH].T
    vf_ref[...] = vt
    for g in range(N_KV_HEADS):
        vt_ref[g * VG_ROWS:g * VG_ROWS + HEAD_DIM, :] = vt[g * HEAD_DIM:(g + 1) * HEAD_DIM].astype(BF16)
        vt_ref[g * VG_ROWS + HEAD_DIM:(g + 1) * VG_ROWS, :] = jnp.ones((VG_ROWS - HEAD_DIM, vt.shape[1]), BF16)
    u_ref[...] = main[:, QPAD + 2 * KV_WIDTH:]

    heads_per_tile = LANES // IDX_DIM
    for j in range(N_IDX_HEADS // heads_per_tile):
        d = _rope(idx[:, j * LANES:(j + 1) * LANES], cos, sin).T
        hi, lo = _split_bf16(d)
        for e in range(heads_per_tile):
            rows = slice(e * IDX_DIM, (e + 1) * IDX_DIM)
            base = 4 * IDX_DIM * (heads_per_tile * j + e)
            for r, part in enumerate((hi, hi, lo, lo)):
                qi4_ref[base + r * IDX_DIM:base + (r + 1) * IDX_DIM, :] = part[rows]
    off = N_IDX_HEADS * IDX_DIM
    raw = idx[:, off:off + LANES]
    d = _rope(raw, cos, sin)
    kif_ref[...] = d.T[:IDX_DIM, :]
    hi = d.astype(BF16).astype(F32)
    lane = lax.broadcasted_iota(I32, d.shape, 1)
    hl = jnp.where(lane < IDX_DIM, hi, pltpu.roll(d - hi, IDX_DIM, 1)).astype(BF16)
    ki4_ref[:, :LANES] = hl
    ki4_ref[:, LANES:] = hl
    misc_ref[...] = raw


def _inproj(x, pos, g_pre, wm, wih, wil, tm):
    B, T, _ = x.shape
    half = HEAD_DIM // 2
    inv = ROPE_THETA ** (-jnp.arange(half, dtype=F32) / half)
    ang = pos.astype(F32)[:, None] * inv[None, :]
    cos = jnp.tile(jnp.cos(ang), (1, LANES // half))
    s = jnp.sin(ang)
    sin = jnp.tile(jnp.concatenate([-s, s], axis=1), (1, LANES // HEAD_DIM))

    def row(w):
        return pl.BlockSpec((None, tm, w), lambda b, t: (b, t, 0))

    def tab():
        return pl.BlockSpec((tm, LANES), lambda b, t: (t, 0))

    def col(w):
        return pl.BlockSpec((None, w, tm), lambda b, t: (b, 0, t))

    out_shapes = (
        jax.ShapeDtypeStruct((B, QPAD, T), BF16),
        jax.ShapeDtypeStruct((B, KV_WIDTH, T), F32),
        jax.ShapeDtypeStruct((B, T, KV_WIDTH), BF16),
        jax.ShapeDtypeStruct((B, KV_WIDTH, T), F32),
        jax.ShapeDtypeStruct((B, VT_ROWS, T), BF16),
        jax.ShapeDtypeStruct((T, B * SSM_WIDTH), F32),
        jax.ShapeDtypeStruct((B, IDX_DIM, T), F32),
        jax.ShapeDtypeStruct((B, T, 4 * IDX_DIM), BF16),
        jax.ShapeDtypeStruct((B, 4 * IDX_DIM * N_IDX_HEADS, T), BF16),
        jax.ShapeDtypeStruct((B, T, LANES), F32),
    )
    out_specs = (
        col(QPAD), col(KV_WIDTH), row(KV_WIDTH), col(KV_WIDTH), col(VT_ROWS),
        pl.BlockSpec((tm, SSM_WIDTH), lambda b, t: (t, b)),
        col(IDX_DIM), row(4 * IDX_DIM), col(4 * IDX_DIM * N_IDX_HEADS), row(LANES),
    )
    return pl.pallas_call(
        _inproj_kernel,
        grid=(B, T // tm),
        in_specs=[row(D_MODEL), _const_spec((1, D_MODEL)), _const_spec((D_MODEL, MAIN_W)),
                  _const_spec((D_MODEL, IDX_W)), _const_spec((D_MODEL, IDX_W)), tab(), tab()],
        out_specs=out_specs,
        out_shape=out_shapes,
        compiler_params=_cparams(2),
        name="inproj",
    )(x, g_pre.reshape(1, D_MODEL), wm, wih, wil, cos, sin)


def _inproj_weights(w_in):
    cols = []
    zero = jnp.zeros((D_MODEL, HEAD_DIM), F32)
    for h in range(N_HEADS):
        wq = w_in[:, OFF_Q + h * HEAD_DIM:OFF_Q + (h + 1) * HEAD_DIM]
        cols += [wq, zero] if h // Q_PER_KV == 0 else [zero, wq]
    cols += [w_in[:, OFF_K:OFF_QI], w_in[:, OFF_U:]]
    wm = jnp.concatenate(cols, axis=1).astype(BF16)
    wi = jnp.concatenate([w_in[:, OFF_QI:OFF_U], jnp.zeros((D_MODEL, IDX_W - (OFF_U - OFF_QI)), F32)], axis=1)
    wih = wi.astype(BF16)
    wil = (wi - wih.astype(F32)).astype(BF16)
    return wm, wih, wil


def _attn_kernel(qi4_ref, wit_ref, ki4_ref, q_ref, kb_ref, vt_ref, o_ref,
                 keys_ref, khi_ref, kraw_ref, klo_ref, qs_ref, lg_ref, p_ref, alpha_ref, tie_ref,
                 m_ref, acc_ref, *, topk, tq):
    i = pl.program_id(1)
    nk = i + 1
    w = wit_ref[...]

    def chunk(c):
        return pl.ds(pl.multiple_of(c * tq, tq), tq)

    def score_rows(rows, diagonal):
        kc = ki4_ref[rows, :]
        sc = None
        for h in range(N_IDX_HEADS):
            s = jnp.dot(kc, qi4_ref[4 * IDX_DIM * h:4 * IDX_DIM * (h + 1), :], preferred_element_type=F32)
            t = jnp.maximum(s, 0.0) * w[h:h + 1, :]
            sc = t if sc is None else sc + t
        key = _sortable(sc)
        if diagonal:
            kpos = lax.broadcasted_iota(I32, key.shape, 0)
            qpos = lax.broadcasted_iota(I32, key.shape, 1)
            key = jnp.where(kpos <= qpos, key, INT_MIN)
        keys_ref[rows, :] = key
        khi_ref[rows, :] = (key >> 16).astype(I16)
        kraw_ref[rows, :] = ((key & 0xFFFF) + I16_MIN).astype(I16)

    def score_pair(cc, _):
        score_rows(chunk(2 * cc), False)
        score_rows(chunk(2 * cc + 1), False)
        return 0

    lax.fori_loop(0, i // 2, score_pair, 0)

    @pl.when(i % 2 == 1)
    def _():
        score_rows(chunk(i - 1), False)

    score_rows(chunk(i), True)

    pack_rows = 2 * SUBLANES
    n_acc = 4

    def count16(ref, cand):
        cand = cand.astype(I16)

        def add_rows(rows, accs):
            m = jnp.where(ref[rows, :] >= cand, jnp.ones((), I16), jnp.zeros((), I16))
            m = m.reshape(m.shape[0] // pack_rows, pack_rows, tq)
            accs = list(accs)
            for r in range(m.shape[0]):
                accs[r % n_acc] = accs[r % n_acc] + m[r]
            return tuple(accs)

        accs = tuple(jnp.zeros((pack_rows, tq), I16) for _ in range(n_acc))
        accs = lax.fori_loop(0, nk // 2, lambda c, a: add_rows(pl.ds(pl.multiple_of(c * 2 * tq, 2 * tq), 2 * tq), a),
                             accs)
        accs = lax.fori_loop(2 * (nk // 2), nk, lambda c, a: add_rows(chunk(c), a), accs)
        tot = (accs[0] + accs[1]) + (accs[2] + accs[3])
        return jnp.sum(tot.astype(I32), axis=0, keepdims=True)

    def search16(ref):
        def bit_body(b, t):
            cand = t + (jnp.int32(1) << (15 - b))
            return jnp.where(count16(ref, cand) >= topk, cand, t)
        return lax.fori_loop(0, 16, bit_body, jnp.full((1, tq), I16_MIN, I32))

    tau_hi = search16(khi_ref)

    tau_hi16 = tau_hi.astype(I16)

    def lo_body(c, _):
        hi = khi_ref[chunk(c), :]
        above = jnp.where(hi > tau_hi16, jnp.full((), I16_MAX, I16), jnp.full((), I16_MIN, I16))
        klo_ref[chunk(c), :] = jnp.where(hi == tau_hi16, kraw_ref[chunk(c), :], above)
        return 0

    lax.fori_loop(0, nk, lo_body, 0)
    tau_lo = search16(klo_ref)
    tau = (tau_hi << 16) + (tau_lo - I16_MIN)
    tau = jnp.maximum(tau, INT_MIN + 1)

    def gt_body(c, acc):
        m = jnp.where(keys_ref[chunk(c), :] > tau, 1, 0).astype(I32)
        return acc + jnp.sum(m.reshape(tq // SUBLANES, SUBLANES, tq), axis=0)

    n_gt = jnp.sum(lax.fori_loop(0, nk, gt_body, jnp.zeros((SUBLANES, tq), I32)), axis=0, keepdims=True)
    quota = (topk - n_gt).astype(F32)

    for h in range(N_HEADS):
        qs_ref[:, h * tq:(h + 1) * tq] = q_ref[h * LANES:(h + 1) * LANES, :]
    m_ref[...] = jnp.full(m_ref.shape, NEG, F32)
    acc_ref[...] = jnp.zeros(acc_ref.shape, F32)
    r_i = lax.broadcasted_iota(I32, (tq, tq), 0)
    c_i = lax.broadcasted_iota(I32, (tq, tq), 1)
    ltri = jnp.where(c_i <= r_i, 1.0, 0.0).astype(BF16)

    def prepare(c, slot):
        cc = jnp.minimum(c, nk - 1)
        kc = keys_ref[chunk(cc), :]
        eq = kc == tau
        e = jnp.where(eq, 1.0, 0.0)
        pre = jnp.dot(ltri, e.astype(BF16), preferred_element_type=F32)
        rank = tie_ref[...] + pre - e
        sel = ((kc > tau) | (eq & (rank < quota))) & (c < nk)
        tie_ref[...] = tie_ref[...] + pre[tq - 1:tq, :]
        bias = jnp.where(sel, 0.0, NEG)
        lg = jnp.dot(kb_ref[chunk(cc), :], qs_ref[...], preferred_element_type=F32)
        for h in range(N_HEADS):
            lg_ref[slot, :, h * tq:(h + 1) * tq] = lg[:, h * tq:(h + 1) * tq] + bias

    def softmax(slot):
        for j in range(N_HEADS * tq // LANES):
            cols = slice(j * LANES, (j + 1) * LANES)
            x = lg_ref[slot, :, cols]
            m_prev = m_ref[:, cols]
            m_new = jnp.maximum(m_prev, jnp.max(x, axis=0, keepdims=True))
            alpha = jnp.exp2(m_prev - m_new)
            m_ref[:, cols] = m_new
            p_ref[slot, :, cols] = jnp.exp2(x - m_new).astype(BF16)
            alpha_ref[slot, :, cols] = alpha

    def value(c, slot):
        cc = jnp.clip(c, 0, nk - 1)
        for g in range(N_KV_HEADS):
            cols = slice(g * Q_PER_KV * tq, (g + 1) * Q_PER_KV * tq)
            pv = jnp.dot(vt_ref[g * VG_ROWS:(g + 1) * VG_ROWS, chunk(cc)], p_ref[slot, :, cols],
                         preferred_element_type=F32)
            acc_ref[:, cols] = alpha_ref[slot, :, cols] * acc_ref[:, cols] + pv

    tie_ref[...] = jnp.zeros(tie_ref.shape, F32)
    p_ref[1] = jnp.zeros(p_ref.shape[1:], BF16)
    alpha_ref[1] = jnp.ones(alpha_ref.shape[1:], F32)
    prepare(0, 0)

    def pair_body(cc, _):
        for slot in range(2):
            c = 2 * cc + slot
            prepare(c + 1, 1 - slot)
            softmax(slot)
            value(c - 1, 1 - slot)
        return 0

    lax.fori_loop(0, nk // 2, pair_body, 0)

    @pl.when(nk % 2 == 1)
    def _():
        softmax(0)
        value(nk - 2, 1)
        value(nk - 1, 0)

    @pl.when(nk % 2 == 0)
    def _():
        value(nk - 1, 1)
    out = acc_ref[:HEAD_DIM, :] / acc_ref[HEAD_DIM:HEAD_DIM + 1, :]
    for j in range(N_HEADS // 2):
        pair = jnp.concatenate([out[:, 2 * j * tq:(2 * j + 1) * tq], out[:, (2 * j + 1) * tq:(2 * j + 2) * tq]], axis=0)
        o_ref[:, j * LANES:(j + 1) * LANES] = pair.T.astype(BF16)


def _prompt_attention(qi4, wit, ki4, q, kb, vt, tq):
    B, T, _ = kb.shape
    topk = min(TOPK_MAX, T // 4)

    def tile(w):
        return pl.BlockSpec((None, tq, w), lambda b, i: (b, i, 0))

    def full(w):
        return pl.BlockSpec((None, T, w), lambda b, i: (b, 0, 0))

    return pl.pallas_call(
        functools.partial(_attn_kernel, topk=topk, tq=tq),
        grid=(B, T // tq),
        in_specs=[pl.BlockSpec((None, 4 * IDX_DIM * N_IDX_HEADS, tq), lambda b, i: (b, 0, i)),
                  pl.BlockSpec((None, N_IDX_HEADS, tq), lambda b, i: (b, 0, i)),
                  full(4 * IDX_DIM), pl.BlockSpec((None, QPAD, tq), lambda b, i: (b, 0, i)), full(KV_WIDTH),
                  pl.BlockSpec((None, VT_ROWS, T), lambda b, i: (b, 0, 0))],
        out_specs=tile(ATTN_WIDTH),
        out_shape=jax.ShapeDtypeStruct((B, T, ATTN_WIDTH), BF16),
        scratch_shapes=[pltpu.VMEM((T, tq), I32), pltpu.VMEM((T, tq), I16), pltpu.VMEM((T, tq), I16),
                        pltpu.VMEM((T, tq), I16), pltpu.VMEM((LANES, N_HEADS * tq), BF16), pltpu.VMEM((2, tq, N_HEADS * tq), F32),
                        pltpu.VMEM((2, tq, N_HEADS * tq), BF16),
                        pltpu.VMEM((2, 1, N_HEADS * tq), F32), pltpu.VMEM((1, tq), F32),
                        pltpu.VMEM((1, N_HEADS * tq), F32), pltpu.VMEM((VG_ROWS, N_HEADS * tq), F32)],
        compiler_params=_cparams(2),
        name="prompt_attention",
    )(qi4, wit, ki4, q, kb, vt)


def _s5_prep_kernel(lr_ref, li_ref, ldt_ref, brt_ref, bit_ref, ar_ref, ai_ref, bbr_ref, bbi_ref):
    lr = lr_ref[...]
    li = li_ref[...]
    dt = jnp.exp(ldt_ref[...])
    mag = jnp.exp(lr * dt)
    ar = mag * jnp.cos(li * dt)
    ai = mag * jnp.sin(li * dt)
    den = lr * lr + li * li
    cr = ((ar - 1.0) * lr + ai * li) / den
    ci = (ai * lr - (ar - 1.0) * li) / den
    ar_ref[...] = ar
    ai_ref[...] = ai
    brt = brt_ref[...]
    bit = bit_ref[...]
    bbr_ref[...] = cr[:, None, :] * brt - ci[:, None, :] * bit
    bbi_ref[...] = cr[:, None, :] * bit + ci[:, None, :] * brt


def _s5_weights(lambda_re, lambda_im, log_dt, b_re, b_im, c_re, c_im):
    G, P, C = N_SSM_GROUPS, SSM_STATE, SSM_GROUP
    ar, ai, bbr, bbi = pl.pallas_call(
        _s5_prep_kernel,
        out_shape=(jax.ShapeDtypeStruct((G, P), F32), jax.ShapeDtypeStruct((G, P), F32),
                   jax.ShapeDtypeStruct((G, C, P), F32), jax.ShapeDtypeStruct((G, C, P), F32)),
        name="s5_prep",
    )(lambda_re, lambda_im, log_dt.reshape(G, 1), jnp.swapaxes(b_re, 1, 2), jnp.swapaxes(b_im, 1, 2))
    eye = jnp.eye(GROUPS_PER_CHUNK, dtype=F32)

    def in_blocks(bb):
        x = bb.reshape(N_UCHUNK, GROUPS_PER_CHUNK, C, P)
        x = x[:, :, :, None, :] * eye[None, :, None, :, None]
        return x.reshape(N_UCHUNK, LANES, GROUPS_PER_CHUNK * P)

    def out_blocks(cc):
        x = jnp.swapaxes(cc, 1, 2).reshape(N_UCHUNK, GROUPS_PER_CHUNK, P, C)
        x = x[:, :, :, None, :] * eye[None, :, None, :, None]
        return x.reshape(N_UCHUNK, GROUPS_PER_CHUNK * P, LANES)

    bblk = jnp.concatenate([in_blocks(bbr), in_blocks(bbi)], axis=2).astype(BF16)
    cblk = jnp.concatenate([out_blocks(c_re), out_blocks(-c_im)], axis=1).astype(BF16)
    half = SLABS_PER_CHUNK // 2

    def slabs(a):
        x = a.reshape(N_UCHUNK, 1, half, 1, LANES)
        return jnp.broadcast_to(x, (N_UCHUNK, 2, half, 1, LANES)).reshape(N_SLAB, 1, LANES)

    return slabs(ar), slabs(ai), bblk, cblk


def _state_to_slabs(h_re, h_im):
    B = h_re.shape[0]
    half = SLABS_PER_CHUNK // 2
    re = h_re.reshape(B, N_UCHUNK, 1, half, LANES)
    im = h_im.reshape(B, N_UCHUNK, 1, half, LANES)
    x = jnp.concatenate([re, im], axis=2).reshape(B, N_SLAB, LANES)
    return jnp.swapaxes(x, 0, 1)


def _slabs_to_state(h):
    B = h.shape[1]
    half = SLABS_PER_CHUNK // 2
    x = jnp.swapaxes(h, 0, 1).reshape(B, N_UCHUNK, 2, half * LANES)
    re = x[:, :, 0].reshape(B, N_SSM_GROUPS, SSM_STATE)
    im = x[:, :, 1].reshape(B, N_SSM_GROUPS, SSM_STATE)
    return re, im


def _s5_kernel(u_ref, h0_ref, ar_ref, ai_ref, bblk_ref, cblk_ref, d_ref, y_ref, hout_ref, bu_ref, hs_ref,
               *, nb, tj, lane_batches):
    rows = tj * nb
    half = SLABS_PER_CHUNK // 2
    blk_rows = rows // lane_batches

    @pl.when(pl.program_id(0) == 0)
    def _():
        hout_ref[...] = h0_ref[...]

    def batch_rows(lb):
        if lane_batches == 1:
            return pl.ds(0, blk_rows)
        return pl.ds(lb, blk_rows, stride=lane_batches)

    def u_cols(lb, q):
        return slice(lb * SSM_WIDTH + q * LANES, lb * SSM_WIDTH + (q + 1) * LANES)

    for q in range(N_UCHUNK):
        ub = jnp.concatenate([u_ref[:, u_cols(lb, q)] for lb in range(lane_batches)], axis=0).astype(BF16)
        bu = jnp.dot(ub, bblk_ref[q], preferred_element_type=F32)
        for lb in range(lane_batches):
            for s in range(SLABS_PER_CHUNK):
                bu_ref[q * SLABS_PER_CHUNK + s, batch_rows(lb), :] = bu[lb * blk_rows:(lb + 1) * blk_rows,
                                                                        s * LANES:(s + 1) * LANES]

    def step(j, h):
        cur = pl.ds(j * nb, nb)
        new = list(h)
        for q in range(N_UCHUNK):
            for r in range(half):
                sr = q * SLABS_PER_CHUNK + r
                si = sr + half
                a_re = ar_ref[sr]
                a_im = ai_ref[sr]
                new[sr] = a_re * h[sr] - a_im * h[si] + bu_ref[sr, cur, :]
                new[si] = a_re * h[si] + a_im * h[sr] + bu_ref[si, cur, :]
                hs_ref[sr, cur, :] = new[sr]
                hs_ref[si, cur, :] = new[si]
        return tuple(new)

    h = lax.fori_loop(0, tj, step, tuple(hout_ref[s] for s in range(N_SLAB)))
    for s in range(N_SLAB):
        hout_ref[s] = h[s]

    for q in range(N_UCHUNK):
        hq = jnp.concatenate(
            [jnp.concatenate([hs_ref[q * SLABS_PER_CHUNK + s, batch_rows(lb), :] for s in range(SLABS_PER_CHUNK)],
                             axis=1) for lb in range(lane_batches)], axis=0).astype(BF16)
        yq = jnp.dot(hq, cblk_ref[q], preferred_element_type=F32)
        for lb in range(lane_batches):
            cols = u_cols(lb, q)
            y_ref[:, cols] = (yq[lb * blk_rows:(lb + 1) * blk_rows]
                              + d_ref[:, q * LANES:(q + 1) * LANES] * u_ref[:, cols])


def _s5_scan(u, nb, lane_batches, h0_slabs, weights, d_skip, tj):
    T = u.shape[0] * lane_batches // nb
    ar, ai, bblk, cblk = weights
    rows = tj * nb
    blk = (rows // lane_batches, lane_batches * SSM_WIDTH)
    y, hout = pl.pallas_call(
        functools.partial(_s5_kernel, nb=nb, tj=tj, lane_batches=lane_batches),
        grid=(T // tj,),
        in_specs=[pl.BlockSpec(blk, lambda t: (t, 0)),
                  _const_spec((N_SLAB, nb, LANES)), _const_spec((N_SLAB, 1, LANES)),
                  _const_spec((N_SLAB, 1, LANES)),
                  _const_spec((N_UCHUNK, LANES, SLABS_PER_CHUNK * LANES)),
                  _const_spec((N_UCHUNK, SLABS_PER_CHUNK * LANES, LANES)),
                  _const_spec((1, SSM_WIDTH))],
        out_specs=(pl.BlockSpec(blk, lambda t: (t, 0)),
                   pl.BlockSpec((N_SLAB, nb, LANES), lambda t: (0, 0, 0))),
        out_shape=(jax.ShapeDtypeStruct(u.shape, F32),
                   jax.ShapeDtypeStruct((N_SLAB, nb, LANES), F32)),
        scratch_shapes=[pltpu.VMEM((N_SLAB, rows, LANES), F32), pltpu.VMEM((N_SLAB, rows, LANES), F32)],
        compiler_params=_cparams(1),
        name="s5_scan",
    )(u, h0_slabs, ar, ai, bblk, cblk, d_skip.reshape(1, SSM_WIDTH))
    return y, hout


def _post_kernel(x_ref, ao_ref, y_ref, wglu_ref, bglu_ref, woa_ref, wos_ref, gpost_ref, gpre_ref,
                 wup_ref, wdn_ref, gfpost_ref, o_ref):
    z = jax.nn.gelu(y_ref[...])
    gate = jax.nn.sigmoid(jnp.dot(z.astype(BF16), wglu_ref[...], preferred_element_type=F32) + bglu_ref[...])
    so = z * gate
    mix = (jnp.dot(ao_ref[...], woa_ref[...], preferred_element_type=F32)
           + jnp.dot(so.astype(BF16), wos_ref[...], preferred_element_type=F32))
    x = x_ref[...] + _rms(mix, gpost_ref[...])
    hdn = jnp.dot(_rms(x, gpre_ref[...]).astype(BF16), wup_ref[...], preferred_element_type=F32)
    hdn = jnp.square(jnp.maximum(hdn, 0.0))
    dn = jnp.dot(hdn.astype(BF16), wdn_ref[...], preferred_element_type=F32)
    o_ref[...] = x + _rms(dn, gfpost_ref[...])


def _post_weights(w_glu, w_out, w_up, w_down):
    return (w_glu.astype(BF16), w_out[:ATTN_WIDTH].astype(BF16), w_out[ATTN_WIDTH:].astype(BF16),
            w_up.astype(BF16), w_down.astype(BF16))


def _post(x, ao, y_tb, weights, b_glu, g_post, g_ffn_pre, g_ffn_post, tm):
    B, T, _ = x.shape
    wglu, woa, wos, wup, wdn = weights

    def row(w):
        return pl.BlockSpec((None, tm, w), lambda b, t: (b, t, 0))

    def vec(a):
        return a.reshape(1, -1)

    return pl.pallas_call(
        _post_kernel,
        grid=(B, T // tm),
        in_specs=[row(D_MODEL), row(ATTN_WIDTH), pl.BlockSpec((tm, SSM_WIDTH), lambda b, t: (t, b)),
                  _const_spec((SSM_WIDTH, SSM_WIDTH)), _const_spec((1, SSM_WIDTH)),
                  _const_spec((ATTN_WIDTH, D_MODEL)), _const_spec((SSM_WIDTH, D_MODEL)),
                  _const_spec((1, D_MODEL)), _const_spec((1, D_MODEL)),
                  _const_spec((D_MODEL, D_FF)), _const_spec((D_FF, D_MODEL)), _const_spec((1, D_MODEL))],
        out_specs=row(D_MODEL),
        out_shape=jax.ShapeDtypeStruct((B, T, D_MODEL), F32),
        compiler_params=_cparams(2),
        name="post",
    )(x, ao, y_tb, wglu, vec(b_glu), woa, wos, vec(g_post), vec(g_ffn_pre), wup, wdn, vec(g_ffn_post))


def _page_map(b, j, pt, *, r, n):
    return (pt[b, j * n + r], 0, 0)


def _skeys_kernel(pt_ref, qh_ref, ql_ref, w_ref, kn_ref, *rest, n_pages_step, dec_t):
    page_refs = rest[:n_pages_step]
    keys_ref, knew_ref = rest[n_pages_step:]
    j = pl.program_id(1)
    qh = qh_ref[...]
    ql = ql_ref[...]
    w = w_ref[...]

    def scores(kc):
        kh, kl = _split_bf16(kc)
        s = (jnp.dot(qh, kh, preferred_element_type=F32) + jnp.dot(ql, kh, preferred_element_type=F32)
             + jnp.dot(qh, kl, preferred_element_type=F32))
        sc = None
        for h in range(N_IDX_HEADS):
            t = jnp.maximum(s[h * dec_t:(h + 1) * dec_t], 0.0) * w[:, h:h + 1]
            sc = t if sc is None else sc + t
        return _sortable(sc)

    keys_ref[...] = scores(jnp.concatenate([r[...] for r in page_refs], axis=1))

    @pl.when(j == pl.num_programs(1) - 1)
    def _():
        kn = scores(kn_ref[...])
        qi_ = lax.broadcasted_iota(I32, kn.shape, 0)
        ki_ = lax.broadcasted_iota(I32, kn.shape, 1)
        knew_ref[...] = jnp.where(ki_ <= qi_, kn, INT_MIN)


def _ssearch_kernel(kp_ref, kn_ref, sel_ref, *, topk, past_len):
    def count(pred_past, pred_new):
        c = jnp.sum(jnp.where(pred_past, 1, 0).astype(I32), axis=1, keepdims=True)
        return c + jnp.sum(jnp.where(pred_new, 1, 0).astype(I32), axis=1, keepdims=True)

    def count_ge(cand):
        return count(kp_ref[...] >= cand, kn_ref[...] >= cand)

    rows = kp_ref.shape[0]
    tau = jnp.where(count_ge(jnp.zeros((rows, 1), I32)) >= topk, 0, INT_MIN).astype(I32)

    def bit_body(b, tau):
        cand = tau + (jnp.int32(1) << (30 - b))
        return jnp.where(count_ge(cand) >= topk, cand, tau)

    tau = lax.fori_loop(0, 31, bit_body, tau)
    tau = jnp.maximum(tau, INT_MIN + 1)
    quota = topk - count(kp_ref[...] > tau, kn_ref[...] > tau)
    pos_p = lax.broadcasted_iota(I32, kp_ref.shape, 1)
    pos_n = past_len + lax.broadcasted_iota(I32, kn_ref.shape, 1)
    n_bits = max(1, (past_len + kn_ref.shape[1] - 1).bit_length())

    def idx_body(b, lim):
        cand = lim + (jnp.int32(1) << (n_bits - 1 - b))
        ties = count((kp_ref[...] == tau) & (pos_p < cand), (kn_ref[...] == tau) & (pos_n < cand))
        return jnp.where(ties < quota, cand, lim)

    lim = lax.fori_loop(0, n_bits, idx_body, jnp.zeros_like(tau))
    kp = kp_ref[...]
    kn = kn_ref[...]
    sel_ref[:, :past_len] = jnp.where((kp > tau) | ((kp == tau) & (pos_p <= lim)), 1.0, 0.0)
    sel_ref[:, past_len:] = jnp.where((kn > tau) | ((kn == tau) & (pos_n <= lim)), 1.0, 0.0)


def _sample_select(page_table, qh, ql, w, ki_new_pad, cache_kidx, n_pages_step):
    DB, n_pages = page_table.shape
    dec_t = w.shape[1]
    past_len = n_pages * PAGE_SIZE
    lp = past_len + LANES
    topk = min(TOPK_MAX, (past_len + dec_t) // 4)
    rows = N_IDX_HEADS * dec_t
    step_keys = n_pages_step * PAGE_SIZE

    def fixed(shape):
        return pl.BlockSpec((None,) + shape, lambda b, j, pt: (b, 0, 0))

    pages = [pl.BlockSpec((None, IDX_DIM, PAGE_SIZE), functools.partial(_page_map, r=r, n=n_pages_step))
             for r in range(n_pages_step)]
    grid_spec = pltpu.PrefetchScalarGridSpec(
        num_scalar_prefetch=1,
        grid=(DB, n_pages // n_pages_step),
        in_specs=[fixed((rows, IDX_DIM)), fixed((rows, IDX_DIM)), fixed((dec_t, N_IDX_HEADS)),
                  fixed((IDX_DIM, LANES))] + pages,
        out_specs=(pl.BlockSpec((None, dec_t, step_keys), lambda b, j, pt: (b, 0, j)), fixed((dec_t, LANES))),
    )
    keys_past, keys_new = pl.pallas_call(
        functools.partial(_skeys_kernel, n_pages_step=n_pages_step, dec_t=dec_t),
        grid_spec=grid_spec,
        out_shape=(jax.ShapeDtypeStruct((DB, dec_t, past_len), I32), jax.ShapeDtypeStruct((DB, dec_t, LANES), I32)),
        compiler_params=_cparams(2),
        name="sample_keys",
    )(page_table, qh, ql, w, ki_new_pad, *([cache_kidx] * n_pages_step))
    sel = pl.pallas_call(
        functools.partial(_ssearch_kernel, topk=topk, past_len=past_len),
        out_shape=jax.ShapeDtypeStruct((DB * dec_t, lp), F32),
        compiler_params=pltpu.CompilerParams(vmem_limit_bytes=VMEM_LIMIT),
        name="sample_search",
    )(keys_past.reshape(DB * dec_t, past_len), keys_new.reshape(DB * dec_t, LANES))
    return sel.reshape(DB, dec_t, lp)


def _sattn_kernel(pt_ref, sel_ref, q_ref, kn_ref, vn_ref, *rest, n_pages_step, past_len, dec_t):
    k_refs = rest[:n_pages_step]
    v_refs = rest[n_pages_step:2 * n_pages_step]
    o_ref, m_ref, l_ref, acc_ref = rest[2 * n_pages_step:]
    j = pl.program_id(1)
    step_keys = n_pages_step * PAGE_SIZE
    q = q_ref[...]

    @pl.when(j == 0)
    def _():
        m_ref[...] = jnp.full(m_ref.shape, NEG, F32)
        l_ref[...] = jnp.zeros(l_ref.shape, F32)
        acc_ref[...] = jnp.zeros(acc_ref.shape, F32)

    def update(kc, vc, sel):
        n = kc.shape[1]
        lg = jnp.dot(q, kc, preferred_element_type=F32)
        lg = jnp.where(sel[None] > 0.5, lg.reshape(N_HEADS, dec_t, n), NEG).reshape(N_HEADS * dec_t, n)
        m_prev = m_ref[...]
        m_new = jnp.maximum(m_prev, jnp.max(lg, axis=1, keepdims=True))
        alpha = jnp.exp2(m_prev - m_new)
        p = jnp.exp2(lg - m_new)
        l_ref[...] = alpha * l_ref[...] + jnp.sum(p, axis=1, keepdims=True)
        acc_ref[...] = alpha * acc_ref[...] + lax.dot_general(p.astype(BF16), vc, NT_DIMS,
                                                              preferred_element_type=F32)
        m_ref[...] = m_new

    kc = jnp.concatenate([r[...] for r in k_refs], axis=1).astype(BF16)
    vc = jnp.concatenate([r[...] for r in v_refs], axis=1).astype(BF16)
    update(kc, vc, sel_ref[:, pl.ds(pl.multiple_of(j * step_keys, step_keys), step_keys)])

    @pl.when(j == pl.num_programs(1) - 1)
    def _():
        update(kn_ref[...], vn_ref[...], sel_ref[:, past_len:])
        o_ref[...] = acc_ref[...] / l_ref[...]


def _sample_attention(page_table, sel, q, k_new_pad, v_new_pad, cache_k, cache_v, n_pages_step):
    DB, n_pages = page_table.shape
    dec_t = sel.shape[1]
    past_len = n_pages * PAGE_SIZE
    lp = past_len + LANES
    rows = N_HEADS * dec_t

    def fixed(shape):
        return pl.BlockSpec((None,) + shape, lambda b, j, pt: (b, 0, 0))

    def pages():
        return [pl.BlockSpec((None, KV_WIDTH, PAGE_SIZE), functools.partial(_page_map, r=r, n=n_pages_step))
                for r in range(n_pages_step)]

    grid_spec = pltpu.PrefetchScalarGridSpec(
        num_scalar_prefetch=1,
        grid=(DB, n_pages // n_pages_step),
        in_specs=[fixed((dec_t, lp)), fixed((rows, KV_WIDTH)), fixed((KV_WIDTH, LANES)),
                  fixed((KV_WIDTH, LANES))] + pages() + pages(),
        out_specs=fixed((rows, KV_WIDTH)),
        scratch_shapes=[pltpu.VMEM((rows, 1), F32), pltpu.VMEM((rows, 1), F32),
                        pltpu.VMEM((rows, KV_WIDTH), F32)],
    )
    return pl.pallas_call(
        functools.partial(_sattn_kernel, n_pages_step=n_pages_step, past_len=past_len, dec_t=dec_t),
        grid_spec=grid_spec,
        out_shape=jax.ShapeDtypeStruct((DB, rows, KV_WIDTH), F32),
        compiler_params=_cparams(2),
        name="sample_attention",
    )(page_table, sel, q, k_new_pad, v_new_pad, *([cache_k] * n_pages_step), *([cache_v] * n_pages_step))


def _largest_tile(n, cap):
    t = min(n, cap)
    while n % t:
        t //= 2
    return t


INPROJ_ROWS = 512
ATTN_TILE = 256
SCAN_STEPS = 256
POST_ROWS = 512
SAMPLE_ROWS = 256
SAMPLE_PAGES_PER_STEP = 32


def _tile_plan(T, n_sample_rows, n_pages):
    return dict(inproj=_largest_tile(T, INPROJ_ROWS), attn=_largest_tile(T, ATTN_TILE),
                scan=_largest_tile(T, SCAN_STEPS), post=_largest_tile(T, POST_ROWS),
                sample=_largest_tile(n_sample_rows, SAMPLE_ROWS),
                pages=_largest_tile(n_pages, SAMPLE_PAGES_PER_STEP))


def _layer(hp, hs, cache_k, cache_v, cache_kidx, st_re, st_im, page_table, g_pre, w_in, ssm_params, d_skip,
           w_glu, b_glu, w_out, g_post, g_ffn_pre, w_up, w_down, g_ffn_post):
    B, T, _ = hp.shape
    DB, dec_t, _ = hs.shape
    n_pages = page_table.shape[1]
    past_len = n_pages * PAGE_SIZE
    n_s = DB * dec_t
    idx_scale = (IDX_DIM ** -0.5) * (N_IDX_HEADS ** -0.5)
    tiles = _tile_plan(T, n_s, n_pages)

    wm, wih, wil = _inproj_weights(w_in)
    s5w = _s5_weights(*ssm_params)
    postw = _post_weights(w_glu, w_out, w_up, w_down)

    pos_p = jnp.arange(T, dtype=I32)
    q, kft, kb, vft, vt, u_tb, kift, ki4, qi4, misc = _inproj(hp, pos_p, g_pre, wm, wih, wil, tiles["inproj"])
    wit = jnp.swapaxes(misc[:, :, IDX_DIM:IDX_DIM + N_IDX_HEADS], 1, 2) * idx_scale
    ao = _prompt_attention(qi4, wit, ki4, q, kb, vt, tiles["attn"])
    h0 = jnp.zeros((N_SLAB, B, LANES), F32)
    y_tb, hfin = _s5_scan(u_tb, B, B, h0, s5w, d_skip, tiles["scan"])
    yp = _post(hp, ao, y_tb, postw, b_glu, g_post, g_ffn_pre, g_ffn_post, tiles["post"])
    srp, sip = _slabs_to_state(hfin)

    def kv_rows(a, nb, t):
        return jnp.transpose(a.reshape(nb, N_KV_HEADS, HEAD_DIM, t), (0, 3, 1, 2))

    prompt_out = (yp, kv_rows(kft, B, T), kv_rows(vft, B, T), jnp.swapaxes(kift, 1, 2), srp, sip)

    pos_s = jnp.tile(past_len + jnp.arange(dec_t, dtype=I32), DB)
    q, kft, _, vft, _, u_s, kift, _, qi4, misc = _inproj(hs.reshape(1, n_s, D_MODEL), pos_s, g_pre, wm, wih, wil,
                                                        tiles["sample"])
    qi4 = jnp.transpose(qi4.reshape(N_IDX_HEADS, 4 * IDX_DIM, DB, dec_t), (2, 0, 3, 1))

    def heads_first(a):
        return a.reshape(DB, a.shape[1] * dec_t, a.shape[3])

    def per_batch(a):
        return jnp.swapaxes(a.reshape(a.shape[1], DB, dec_t), 0, 1)

    def pad_new(a, dtype):
        return jnp.pad(a, ((0, 0), (0, 0), (0, LANES - dec_t))).astype(dtype)

    k_s, v_s, ki_s = per_batch(kft), per_batch(vft), per_batch(kift)
    qh = heads_first(qi4[..., :IDX_DIM])
    ql = heads_first(qi4[..., 2 * IDX_DIM:3 * IDX_DIM])
    w_s = misc.reshape(DB, dec_t, LANES)[:, :, IDX_DIM:IDX_DIM + N_IDX_HEADS] * idx_scale

    n_phys = cache_k.shape[0]
    pool_ki = jnp.swapaxes(cache_kidx, 1, 2)
    pool_k = jnp.transpose(cache_k, (0, 2, 3, 1)).reshape(n_phys, KV_WIDTH, PAGE_SIZE)
    pool_v = jnp.transpose(cache_v, (0, 2, 3, 1)).reshape(n_phys, KV_WIDTH, PAGE_SIZE)
    n_step = tiles["pages"]
    sel = _sample_select(page_table, qh, ql, w_s, pad_new(ki_s, F32), pool_ki, n_step)
    qs = jnp.transpose(q.reshape(N_HEADS, LANES, DB, dec_t), (2, 0, 3, 1)).reshape(DB, N_HEADS * dec_t, LANES)
    ao_s = _sample_attention(page_table, sel, qs, pad_new(k_s, BF16), pad_new(v_s, BF16), pool_k, pool_v, n_step)
    ao_s = ao_s.reshape(DB, N_KV_HEADS, Q_PER_KV, dec_t, N_KV_HEADS, HEAD_DIM)
    ao_s = jnp.stack([ao_s[:, g, :, :, g] for g in range(N_KV_HEADS)], axis=1).reshape(DB, N_HEADS, dec_t, HEAD_DIM)
    ao_s = jnp.swapaxes(ao_s, 1, 2).reshape(1, n_s, ATTN_WIDTH).astype(BF16)
    u_t = jnp.swapaxes(u_s.reshape(DB, dec_t, SSM_WIDTH), 0, 1).reshape(n_s, SSM_WIDTH)
    y_t, hfin = _s5_scan(u_t, DB, 1, _state_to_slabs(st_re, st_im), s5w, d_skip, dec_t)
    y_s = jnp.swapaxes(y_t.reshape(dec_t, DB, SSM_WIDTH), 0, 1).reshape(n_s, SSM_WIDTH)
    ys = _post(hs.reshape(1, n_s, D_MODEL), ao_s, y_s, postw, b_glu, g_post, g_ffn_pre, g_ffn_post,
               tiles["sample"])
    srs, sis = _slabs_to_state(hfin)
    sample_out = (ys.reshape(DB, dec_t, D_MODEL), kv_rows(k_s, DB, dec_t), kv_rows(v_s, DB, dec_t),
                  jnp.swapaxes(ki_s, 1, 2), srs, sis)
    return prompt_out, sample_out


def kernel(x_prompt, x_sample, cache_k, cache_v, cache_kidx, state_ssm_re, state_ssm_im, page_table, norm_mix_pre, w_in, lambda_re, lambda_im, log_dt, b_re, b_im, c_re, c_im, d_skip, w_glu, b_glu, w_out, norm_mix_post, norm_ffn_pre, w_up, w_down, norm_ffn_post):
    depth = w_in.shape[0]
    hp, hs = x_prompt, x_sample
    p_acc = [[] for _ in range(5)]
    s_acc = [[] for _ in range(5)]
    for l in range(depth):
        ssm_params = (lambda_re[l], lambda_im[l], log_dt[l], b_re[l], b_im[l], c_re[l], c_im[l])
        p_out, s_out = _layer(hp, hs, cache_k[l], cache_v[l], cache_kidx[l], state_ssm_re[l], state_ssm_im[l],
                              page_table, norm_mix_pre[l], w_in[l], ssm_params, d_skip[l], w_glu[l], b_glu[l],
                              w_out[l], norm_mix_post[l], norm_ffn_pre[l], w_up[l], w_down[l], norm_ffn_post[l])
        hp, hs = p_out[0], s_out[0]
        for acc, vals in ((p_acc, p_out[1:]), (s_acc, s_out[1:])):
            for a, v in zip(acc, vals):
                a.append(v)
    return (hp, hs) + tuple(jnp.stack(a) for a in p_acc) + tuple(jnp.stack(a) for a in s_acc)
```

```python
import functools
import math

import jax
import jax.numpy as jnp
from jax import lax
from jax.experimental import pallas as pl
from jax.experimental.pallas import tpu as pltpu

F32 = jnp.float32
BF16 = jnp.bfloat16
I32 = jnp.int32

D_MODEL = 1024
PAGE_SIZE = 128
HEAD_DIM = 64
ATTN_WIDTH = D_MODEL // 2
N_HEADS = ATTN_WIDTH // HEAD_DIM
N_KV_HEADS = 2
Q_PER_KV = N_HEADS // N_KV_HEADS
KV_WIDTH = N_KV_HEADS * HEAD_DIM
N_IDX_HEADS = 4
IDX_DIM = 64
TOPK_MAX = 256
ROPE_THETA = 10000.0
SSM_WIDTH = D_MODEL - ATTN_WIDTH
SSM_GROUP = 16
N_SSM_GROUPS = SSM_WIDTH // SSM_GROUP
SSM_STATE = 64
D_FF = 4 * D_MODEL
RMS_EPS = 1e-6
NEG = -1e30
OFF_Q = 0
OFF_K = OFF_Q + ATTN_WIDTH
OFF_V = OFF_K + KV_WIDTH
OFF_QI = OFF_V + KV_WIDTH
OFF_KI = OFF_QI + N_IDX_HEADS * IDX_DIM
OFF_WI = OFF_KI + IDX_DIM
OFF_U = OFF_WI + N_IDX_HEADS
IN_WIDTH = OFF_U + SSM_WIDTH

LANES = 128
SUBLANES = 8
INT_MIN = -(2 ** 31)
VMEM_LIMIT = 56 * 1024 * 1024

QPAD = N_HEADS * LANES
MAIN_W = QPAD + 2 * KV_WIDTH + SSM_WIDTH
IDX_W = N_IDX_HEADS * IDX_DIM + LANES
SSM_LANES = N_SSM_GROUPS * SSM_STATE
N_SLAB = 2 * SSM_LANES // LANES
N_UCHUNK = SSM_WIDTH // LANES
GROUPS_PER_CHUNK = LANES // SSM_GROUP
SLABS_PER_CHUNK = N_SLAB // N_UCHUNK

NT_DIMS = (((1,), (1,)), ((), ()))
LOG2_E = math.log2(math.e)
I16 = jnp.int16
I16_MIN = -(2 ** 15)
I16_MAX = 2 ** 15 - 1
VG_ROWS = HEAD_DIM + 2 * SUBLANES
VT_ROWS = N_KV_HEADS * VG_ROWS


def _cparams(n_grid):
    return pltpu.CompilerParams(dimension_semantics=("arbitrary",) * n_grid,
                                vmem_limit_bytes=VMEM_LIMIT)


def _const_spec(shape):
    zeros = (0,) * len(shape)
    return pl.BlockSpec(shape, lambda *_: zeros, pipeline_mode=pl.Buffered(1))


def _rms(x, g):
    return x * lax.rsqrt(jnp.mean(x * x, axis=-1, keepdims=True) + RMS_EPS) * g


def _split_bf16(x):
    hi = x.astype(BF16)
    lo = (x - hi.astype(F32)).astype(BF16)
    return hi, lo


def _sortable(x):
    b = lax.bitcast_convert_type(x, I32)
    return b ^ ((b >> 31) & jnp.int32(0x7FFFFFFF))


def _rope(x, cos, sin):
    w = x.shape[-1]
    lane = lax.broadcasted_iota(I32, x.shape, 1)
    first = (lane & (HEAD_DIM - 1)) < HEAD_DIM // 2
    sw = jnp.where(first, pltpu.roll(x, w - HEAD_DIM // 2, 1), pltpu.roll(x, HEAD_DIM // 2, 1))
    return x * cos + sw * sin


def _inproj_kernel(x_ref, g_ref, wm_ref, wih_ref, wil_ref, cos_ref, sin_ref,
                   q_ref, kf_ref, kb_ref, vf_ref, vt_ref, u_ref, kif_ref, ki4_ref, qi4_ref, misc_ref):
    xn = _rms(x_ref[...], g_ref[...])
    xh, xl = _split_bf16(xn)
    cos = cos_ref[...]
    sin = sin_ref[...]
    main = jnp.dot(xh, wm_ref[...], preferred_element_type=F32)
    wih = wih_ref[...]
    idx = (jnp.dot(xh, wih, preferred_element_type=F32) + jnp.dot(xl, wih, preferred_element_type=F32)
           + jnp.dot(xh, wil_ref[...], preferred_element_type=F32))

    scale = HEAD_DIM ** -0.5 * LOG2_E
    for h in range(N_HEADS):
        blk = main[:, h * LANES:(h + 1) * LANES]
        q_ref[h * LANES:(h + 1) * LANES, :] = (_rope(blk, cos, sin) * scale).T.astype(BF16)
    k = _rope(main[:, QPAD:QPAD + KV_WIDTH], cos, sin)
    kf_ref[...] = k.T
    kb_ref[...] = k.astype(BF16)
    vt = main[:, QPAD + KV_WIDTH:QPAD + 2 * KV_WIDTH].T
    vf_ref[...] = vt
    for g in range(N_KV_HEADS):
        vt_ref[g * VG_ROWS:g * VG_ROWS + HEAD_DIM, :] = vt[g * HEAD_DIM:(g + 1) * HEAD_DIM].astype(BF16)
        vt_ref[g * VG_ROWS + HEAD_DIM:(g + 1) * VG_ROWS, :] = jnp.ones((VG_ROWS - HEAD_DIM, vt.shape[1]), BF16)
    u_ref[...] = main[:, QPAD + 2 * KV_WIDTH:]

    heads_per_tile = LANES // IDX_DIM
    for j in range(N_IDX_HEADS // heads_per_tile):
        d = _rope(idx[:, j * LANES:(j + 1) * LANES], cos, sin).T
        hi, lo = _split_bf16(d)
        for e in range(heads_per_tile):
            rows = slice(e * IDX_DIM, (e + 1) * IDX_DIM)
            base = 4 * IDX_DIM * (heads_per_tile * j + e)
            for r, part in enumerate((hi, hi, lo, lo)):
                qi4_ref[base + r * IDX_DIM:base + (r + 1) * IDX_DIM, :] = part[rows]
    off = N_IDX_HEADS * IDX_DIM
    raw = idx[:, off:off + LANES]
    d = _rope(raw, cos, sin)
    kif_ref[...] = d.T[:IDX_DIM, :]
    hi = d.astype(BF16).astype(F32)
    lane = lax.broadcasted_iota(I32, d.shape, 1)
    hl = jnp.where(lane < IDX_DIM, hi, pltpu.roll(d - hi, IDX_DIM, 1)).astype(BF16)
    ki4_ref[:, :LANES] = hl
    ki4_ref[:, LANES:] = hl
    misc_ref[...] = raw


def _inproj(x, pos, g_pre, wm, wih, wil, tm):
    B, T, _ = x.shape
    half = HEAD_DIM // 2
    inv = ROPE_THETA ** (-jnp.arange(half, dtype=F32) / half)
    ang = pos.astype(F32)[:, None] * inv[None, :]
    cos = jnp.tile(jnp.cos(ang), (1, LANES // half))
    s = jnp.sin(ang)
    sin = jnp.tile(jnp.concatenate([-s, s], axis=1), (1, LANES // HEAD_DIM))

    def row(w):
        return pl.BlockSpec((None, tm, w), lambda b, t: (b, t, 0))

    def tab():
        return pl.BlockSpec((tm, LANES), lambda b, t: (t, 0))

    def col(w):
        return pl.BlockSpec((None, w, tm), lambda b, t: (b, 0, t))

    out_shapes = (
        jax.ShapeDtypeStruct((B, QPAD, T), BF16),
        jax.ShapeDtypeStruct((B, KV_WIDTH, T), F32),
        jax.ShapeDtypeStruct((B, T, KV_WIDTH), BF16),
        jax.ShapeDtypeStruct((B, KV_WIDTH, T), F32),
        jax.ShapeDtypeStruct((B, VT_ROWS, T), BF16),
        jax.ShapeDtypeStruct((T, B * SSM_WIDTH), F32),
        jax.ShapeDtypeStruct((B, IDX_DIM, T), F32),
        jax.ShapeDtypeStruct((B, T, 4 * IDX_DIM), BF16),
        jax.ShapeDtypeStruct((B, 4 * IDX_DIM * N_IDX_HEADS, T), BF16),
        jax.ShapeDtypeStruct((B, T, LANES), F32),
    )
    out_specs = (
        col(QPAD), col(KV_WIDTH), row(KV_WIDTH), col(KV_WIDTH), col(VT_ROWS),
        pl.BlockSpec((tm, SSM_WIDTH), lambda b, t: (t, b)),
        col(IDX_DIM), row(4 * IDX_DIM), col(4 * IDX_DIM * N_IDX_HEADS), row(LANES),
    )
    return pl.pallas_call(
        _inproj_kernel,
        grid=(B, T // tm),
        in_specs=[row(D_MODEL), _const_spec((1, D_MODEL)), _const_spec((D_MODEL, MAIN_W)),
                  _const_spec((D_MODEL, IDX_W)), _const_spec((D_MODEL, IDX_W)), tab(), tab()],
        out_specs=out_specs,
        out_shape=out_shapes,
        compiler_params=_cparams(2),
        name="inproj",
    )(x, g_pre.reshape(1, D_MODEL), wm, wih, wil, cos, sin)


def _inproj_weights(w_in):
    cols = []
    zero = jnp.zeros((D_MODEL, HEAD_DIM), F32)
    for h in range(N_HEADS):
        wq = w_in[:, OFF_Q + h * HEAD_DIM:OFF_Q + (h + 1) * HEAD_DIM]
        cols += [wq, zero] if h // Q_PER_KV == 0 else [zero, wq]
    cols += [w_in[:, OFF_K:OFF_QI], w_in[:, OFF_U:]]
    wm = jnp.concatenate(cols, axis=1).astype(BF16)
    wi = jnp.concatenate([w_in[:, OFF_QI:OFF_U], jnp.zeros((D_MODEL, IDX_W - (OFF_U - OFF_QI)), F32)], axis=1)
    wih = wi.astype(BF16)
    wil = (wi - wih.astype(F32)).astype(BF16)
    return wm, wih, wil


def _attn_kernel(qi4_ref, wit_ref, ki4_ref, q_ref, kb_ref, vt_ref, o_ref,
                 keys_ref, khi_ref, kraw_ref, klo_ref, qs_ref, lg_ref, p_ref, alpha_ref, tie_ref,
                 m_ref, acc_ref, *, topk, tq):
    i = pl.program_id(1)
    nk = i + 1
    w = wit_ref[...]

    def chunk(c):
        return pl.ds(pl.multiple_of(c * tq, tq), tq)

    def score_rows(rows, diagonal):
        kc = ki4_ref[rows, :]
        sc = None
        for h in range(N_IDX_HEADS):
            s = jnp.dot(kc, qi4_ref[4 * IDX_DIM * h:4 * IDX_DIM * (h + 1), :], preferred_element_type=F32)
            t = jnp.maximum(s, 0.0) * w[h:h + 1, :]
            sc = t if sc is None else sc + t
        key = _sortable(sc)
        if diagonal:
            kpos = lax.broadcasted_iota(I32, key.shape, 0)
            qpos = lax.broadcasted_iota(I32, key.shape, 1)
            key = jnp.where(kpos <= qpos, key, INT_MIN)
        keys_ref[rows, :] = key
        khi_ref[rows, :] = (key >> 16).astype(I16)
        kraw_ref[rows, :] = ((key & 0xFFFF) + I16_MIN).astype(I16)

    def score_pair(cc, _):
        score_rows(chunk(2 * cc), False)
        score_rows(chunk(2 * cc + 1), False)
        return 0

    lax.fori_loop(0, i // 2, score_pair, 0)

    @pl.when(i % 2 == 1)
    def _():
        score_rows(chunk(i - 1), False)

    score_rows(chunk(i), True)

    pack_rows = 2 * SUBLANES
    n_acc = 4

    def count16(ref, cand):
        cand = cand.astype(I16)

        def add_rows(rows, accs):
            m = jnp.where(ref[rows, :] >= cand, jnp.ones((), I16), jnp.zeros((), I16))
            m = m.reshape(m.shape[0] // pack_rows, pack_rows, tq)
            accs = list(accs)
            for r in range(m.shape[0]):
                accs[r % n_acc] = accs[r % n_acc] + m[r]
            return tuple(accs)

        accs = tuple(jnp.zeros((pack_rows, tq), I16) for _ in range(n_acc))
        accs = lax.fori_loop(0, nk // 2, lambda c, a: add_rows(pl.ds(pl.multiple_of(c * 2 * tq, 2 * tq), 2 * tq), a),
                             accs)
        accs = lax.fori_loop(2 * (nk // 2), nk, lambda c, a: add_rows(chunk(c), a), accs)
        tot = (accs[0] + accs[1]) + (accs[2] + accs[3])
        return jnp.sum(tot.astype(I32), axis=0, keepdims=True)

    def search16(ref):
        def bit_body(b, t):
            cand = t + (jnp.int32(1) << (15 - b))
            return jnp.where(count16(ref, cand) >= topk, cand, t)
        return lax.fori_loop(0, 16, bit_body, jnp.full((1, tq), I16_MIN, I32))

    tau_hi = search16(khi_ref)

    tau_hi16 = tau_hi.astype(I16)

    def lo_body(c, _):
        hi = khi_ref[chunk(c), :]
        above = jnp.where(hi > tau_hi16, jnp.full((), I16_MAX, I16), jnp.full((), I16_MIN, I16))
        klo_ref[chunk(c), :] = jnp.where(hi == tau_hi16, kraw_ref[chunk(c), :], above)
        return 0

    lax.fori_loop(0, nk, lo_body, 0)
    tau_lo = search16(klo_ref)
    tau = (tau_hi << 16) + (tau_lo - I16_MIN)
    tau = jnp.maximum(tau, INT_MIN + 1)

    def gt_body(c, acc):
        m = jnp.where(keys_ref[chunk(c), :] > tau, 1, 0).astype(I32)
        return acc + jnp.sum(m.reshape(tq // SUBLANES, SUBLANES, tq), axis=0)

    n_gt = jnp.sum(lax.fori_loop(0, nk, gt_body, jnp.zeros((SUBLANES, tq), I32)), axis=0, keepdims=True)
    quota = (topk - n_gt).astype(F32)

    for h in range(N_HEADS):
        qs_ref[:, h * tq:(h + 1) * tq] = q_ref[h * LANES:(h + 1) * LANES, :]
    m_ref[...] = jnp.full(m_ref.shape, NEG, F32)
    acc_ref[...] = jnp.zeros(acc_ref.shape, F32)
    r_i = lax.broadcasted_iota(I32, (tq, tq), 0)
    c_i = lax.broadcasted_iota(I32, (tq, tq), 1)
    ltri = jnp.where(c_i <= r_i, 1.0, 0.0).astype(BF16)

    def prepare(c, slot):
        cc = jnp.minimum(c, nk - 1)
        kc = keys_ref[chunk(cc), :]
        eq = kc == tau
        e = jnp.where(eq, 1.0, 0.0)
        pre = jnp.dot(ltri, e.astype(BF16), preferred_element_type=F32)
        rank = tie_ref[...] + pre - e
        sel = ((kc > tau) | (eq & (rank < quota))) & (c < nk)
        tie_ref[...] = tie_ref[...] + pre[tq - 1:tq, :]
        bias = jnp.where(sel, 0.0, NEG)
        lg = jnp.dot(kb_ref[chunk(cc), :], qs_ref[...], preferred_element_type=F32)
        for h in range(N_HEADS):
            lg_ref[slot, :, h * tq:(h + 1) * tq] = lg[:, h * tq:(h + 1) * tq] + bias

    def softmax(slot):
        for j in range(N_HEADS * tq // LANES):
            cols = slice(j * LANES, (j + 1) * LANES)
            x = lg_ref[slot, :, cols]
            m_prev = m_ref[:, cols]
            m_new = jnp.maximum(m_prev, jnp.max(x, axis=0, keepdims=True))
            alpha = jnp.exp2(m_prev - m_new)
            m_ref[:, cols] = m_new
            p_ref[slot, :, cols] = jnp.exp2(x - m_new).astype(BF16)
            alpha_ref[slot, :, cols] = alpha

    def value(c, slot):
        cc = jnp.clip(c, 0, nk - 1)
        for g in range(N_KV_HEADS):
            cols = slice(g * Q_PER_KV * tq, (g + 1) * Q_PER_KV * tq)
            pv = jnp.dot(vt_ref[g * VG_ROWS:(g + 1) * VG_ROWS, chunk(cc)], p_ref[slot, :, cols],
                         preferred_element_type=F32)
            acc_ref[:, cols] = alpha_ref[slot, :, cols] * acc_ref[:, cols] + pv

    tie_ref[...] = jnp.zeros(tie_ref.shape, F32)

    def chunk_step(c, slot):
        prepare(c, slot)
        softmax(slot)
        value(c, slot)

    def pair_body(cc, _):
        for slot in range(2):
            chunk_step(2 * cc + slot, slot)
        return 0

    lax.fori_loop(0, nk // 2, pair_body, 0)

    @pl.when(nk % 2 == 1)
    def _():
        chunk_step(nk - 1, 0)
    out = acc_ref[:HEAD_DIM, :] / acc_ref[HEAD_DIM:HEAD_DIM + 1, :]
    for j in range(N_HEADS // 2):
        pair = jnp.concatenate([out[:, 2 * j * tq:(2 * j + 1) * tq], out[:, (2 * j + 1) * tq:(2 * j + 2) * tq]], axis=0)
        o_ref[:, j * LANES:(j + 1) * LANES] = pair.T.astype(BF16)


def _prompt_attention(qi4, wit, ki4, q, kb, vt, tq):
    B, T, _ = kb.shape
    topk = min(TOPK_MAX, T // 4)

    def tile(w):
        return pl.BlockSpec((None, tq, w), lambda b, i: (b, i, 0))

    def full(w):
        return pl.BlockSpec((None, T, w), lambda b, i: (b, 0, 0))

    return pl.pallas_call(
        functools.partial(_attn_kernel, topk=topk, tq=tq),
        grid=(B, T // tq),
        in_specs=[pl.BlockSpec((None, 4 * IDX_DIM * N_IDX_HEADS, tq), lambda b, i: (b, 0, i)),
                  pl.BlockSpec((None, N_IDX_HEADS, tq), lambda b, i: (b, 0, i)),
                  full(4 * IDX_DIM), pl.BlockSpec((None, QPAD, tq), lambda b, i: (b, 0, i)), full(KV_WIDTH),
                  pl.BlockSpec((None, VT_ROWS, T), lambda b, i: (b, 0, 0))],
        out_specs=tile(ATTN_WIDTH),
        out_shape=jax.ShapeDtypeStruct((B, T, ATTN_WIDTH), BF16),
        scratch_shapes=[pltpu.VMEM((T, tq), I32), pltpu.VMEM((T, tq), I16), pltpu.VMEM((T, tq), I16),
                        pltpu.VMEM((T, tq), I16), pltpu.VMEM((LANES, N_HEADS * tq), BF16), pltpu.VMEM((2, tq, N_HEADS * tq), F32),
                        pltpu.VMEM((2, tq, N_HEADS * tq), BF16),
                        pltpu.VMEM((2, 1, N_HEADS * tq), F32), pltpu.VMEM((1, tq), F32),
                        pltpu.VMEM((1, N_HEADS * tq), F32), pltpu.VMEM((VG_ROWS, N_HEADS * tq), F32)],
        compiler_params=_cparams(2),
        name="prompt_attention",
    )(qi4, wit, ki4, q, kb, vt)


def _s5_prep_kernel(lr_ref, li_ref, ldt_ref, brt_ref, bit_ref, ar_ref, ai_ref, bbr_ref, bbi_ref):
    lr = lr_ref[...]
    li = li_ref[...]
    dt = jnp.exp(ldt_ref[...])
    mag = jnp.exp(lr * dt)
    ar = mag * jnp.cos(li * dt)
    ai = mag * jnp.sin(li * dt)
    den = lr * lr + li * li
    cr = ((ar - 1.0) * lr + ai * li) / den
    ci = (ai * lr - (ar - 1.0) * li) / den
    ar_ref[...] = ar
    ai_ref[...] = ai
    brt = brt_ref[...]
    bit = bit_ref[...]
    bbr_ref[...] = cr[:, None, :] * brt - ci[:, None, :] * bit
    bbi_ref[...] = cr[:, None, :] * bit + ci[:, None, :] * brt


def _s5_weights(lambda_re, lambda_im, log_dt, b_re, b_im, c_re, c_im):
    G, P, C = N_SSM_GROUPS, SSM_STATE, SSM_GROUP
    ar, ai, bbr, bbi = pl.pallas_call(
        _s5_prep_kernel,
        out_shape=(jax.ShapeDtypeStruct((G, P), F32), jax.ShapeDtypeStruct((G, P), F32),
                   jax.ShapeDtypeStruct((G, C, P), F32), jax.ShapeDtypeStruct((G, C, P), F32)),
        name="s5_prep",
    )(lambda_re, lambda_im, log_dt.reshape(G, 1), jnp.swapaxes(b_re, 1, 2), jnp.swapaxes(b_im, 1, 2))
    eye = jnp.eye(GROUPS_PER_CHUNK, dtype=F32)

    def in_blocks(bb):
        x = bb.reshape(N_UCHUNK, GROUPS_PER_CHUNK, C, P)
        x = x[:, :, :, None, :] * eye[None, :, None, :, None]
        return x.reshape(N_UCHUNK, LANES, GROUPS_PER_CHUNK * P)

    def out_blocks(cc):
        x = jnp.swapaxes(cc, 1, 2).reshape(N_UCHUNK, GROUPS_PER_CHUNK, P, C)
        x = x[:, :, :, None, :] * eye[None, :, None, :, None]
        return x.reshape(N_UCHUNK, GROUPS_PER_CHUNK * P, LANES)

    bblk = jnp.concatenate([in_blocks(bbr), in_blocks(bbi)], axis=2).astype(BF16)
    cblk = jnp.concatenate([out_blocks(c_re), out_blocks(-c_im)], axis=1).astype(BF16)
    half = SLABS_PER_CHUNK // 2

    def slabs(a):
        x = a.reshape(N_UCHUNK, 1, half, 1, LANES)
        return jnp.broadcast_to(x, (N_UCHUNK, 2, half, 1, LANES)).reshape(N_SLAB, 1, LANES)

    return slabs(ar), slabs(ai), bblk, cblk


def _state_to_slabs(h_re, h_im):
    B = h_re.shape[0]
    half = SLABS_PER_CHUNK // 2
    re = h_re.reshape(B, N_UCHUNK, 1, half, LANES)
    im = h_im.reshape(B, N_UCHUNK, 1, half, LANES)
    x = jnp.concatenate([re, im], axis=2).reshape(B, N_SLAB, LANES)
    return jnp.swapaxes(x, 0, 1)


def _slabs_to_state(h):
    B = h.shape[1]
    half = SLABS_PER_CHUNK // 2
    x = jnp.swapaxes(h, 0, 1).reshape(B, N_UCHUNK, 2, half * LANES)
    re = x[:, :, 0].reshape(B, N_SSM_GROUPS, SSM_STATE)
    im = x[:, :, 1].reshape(B, N_SSM_GROUPS, SSM_STATE)
    return re, im


def _s5_kernel(u_ref, h0_ref, ar_ref, ai_ref, bblk_ref, cblk_ref, d_ref, y_ref, hout_ref, bu_ref, hs_ref,
               *, nb, tj, lane_batches):
    rows = tj * nb
    half = SLABS_PER_CHUNK // 2
    blk_rows = rows // lane_batches

    @pl.when(pl.program_id(0) == 0)
    def _():
        hout_ref[...] = h0_ref[...]

    def batch_rows(lb):
        if lane_batches == 1:
            return pl.ds(0, blk_rows)
        return pl.ds(lb, blk_rows, stride=lane_batches)

    def u_cols(lb, q):
        return slice(lb * SSM_WIDTH + q * LANES, lb * SSM_WIDTH + (q + 1) * LANES)

    for q in range(N_UCHUNK):
        ub = jnp.concatenate([u_ref[:, u_cols(lb, q)] for lb in range(lane_batches)], axis=0).astype(BF16)
        bu = jnp.dot(ub, bblk_ref[q], preferred_element_type=F32)
        for lb in range(lane_batches):
            for s in range(SLABS_PER_CHUNK):
                bu_ref[q * SLABS_PER_CHUNK + s, batch_rows(lb), :] = bu[lb * blk_rows:(lb + 1) * blk_rows,
                                                                        s * LANES:(s + 1) * LANES]

    def step(j, h):
        cur = pl.ds(j * nb, nb)
        new = list(h)
        for q in range(N_UCHUNK):
            for r in range(half):
                sr = q * SLABS_PER_CHUNK + r
                si = sr + half
                a_re = ar_ref[sr]
                a_im = ai_ref[sr]
                new[sr] = a_re * h[sr] - a_im * h[si] + bu_ref[sr, cur, :]
                new[si] = a_re * h[si] + a_im * h[sr] + bu_ref[si, cur, :]
                hs_ref[sr, cur, :] = new[sr]
                hs_ref[si, cur, :] = new[si]
        return tuple(new)

    h = lax.fori_loop(0, tj, step, tuple(hout_ref[s] for s in range(N_SLAB)))
    for s in range(N_SLAB):
        hout_ref[s] = h[s]

    for q in range(N_UCHUNK):
        hq = jnp.concatenate(
            [jnp.concatenate([hs_ref[q * SLABS_PER_CHUNK + s, batch_rows(lb), :] for s in range(SLABS_PER_CHUNK)],
                             axis=1) for lb in range(lane_batches)], axis=0).astype(BF16)
        yq = jnp.dot(hq, cblk_ref[q], preferred_element_type=F32)
        for lb in range(lane_batches):
            cols = u_cols(lb, q)
            y_ref[:, cols] = (yq[lb * blk_rows:(lb + 1) * blk_rows]
                              + d_ref[:, q * LANES:(q + 1) * LANES] * u_ref[:, cols])


def _s5_scan(u, nb, lane_batches, h0_slabs, weights, d_skip, tj):
    T = u.shape[0] * lane_batches // nb
    ar, ai, bblk, cblk = weights
    rows = tj * nb
    blk = (rows // lane_batches, lane_batches * SSM_WIDTH)
    y, hout = pl.pallas_call(
        functools.partial(_s5_kernel, nb=nb, tj=tj, lane_batches=lane_batches),
        grid=(T // tj,),
        in_specs=[pl.BlockSpec(blk, lambda t: (t, 0)),
                  _const_spec((N_SLAB, nb, LANES)), _const_spec((N_SLAB, 1, LANES)),
                  _const_spec((N_SLAB, 1, LANES)),
                  _const_spec((N_UCHUNK, LANES, SLABS_PER_CHUNK * LANES)),
                  _const_spec((N_UCHUNK, SLABS_PER_CHUNK * LANES, LANES)),
                  _const_spec((1, SSM_WIDTH))],
        out_specs=(pl.BlockSpec(blk, lambda t: (t, 0)),
                   pl.BlockSpec((N_SLAB, nb, LANES), lambda t: (0, 0, 0))),
        out_shape=(jax.ShapeDtypeStruct(u.shape, F32),
                   jax.ShapeDtypeStruct((N_SLAB, nb, LANES), F32)),
        scratch_shapes=[pltpu.VMEM((N_SLAB, rows, LANES), F32), pltpu.VMEM((N_SLAB, rows, LANES), F32)],
        compiler_params=_cparams(1),
        name="s5_scan",
    )(u, h0_slabs, ar, ai, bblk, cblk, d_skip.reshape(1, SSM_WIDTH))
    return y, hout


def _post_kernel(x_ref, ao_ref, y_ref, wglu_ref, bglu_ref, woa_ref, wos_ref, gpost_ref, gpre_ref,
                 wup_ref, wdn_ref, gfpost_ref, o_ref):
    z = jax.nn.gelu(y_ref[...])
    gate = jax.nn.sigmoid(jnp.dot(z.astype(BF16), wglu_ref[...], preferred_element_type=F32) + bglu_ref[...])
    so = z * gate
    mix = (jnp.dot(ao_ref[...], woa_ref[...], preferred_element_type=F32)
           + jnp.dot(so.astype(BF16), wos_ref[...], preferred_element_type=F32))
    x = x_ref[...] + _rms(mix, gpost_ref[...])
    hdn = jnp.dot(_rms(x, gpre_ref[...]).astype(BF16), wup_ref[...], preferred_element_type=F32)
    hdn = jnp.square(jnp.maximum(hdn, 0.0))
    dn = jnp.dot(hdn.astype(BF16), wdn_ref[...], preferred_element_type=F32)
    o_ref[...] = x + _rms(dn, gfpost_ref[...])


def _post_weights(w_glu, w_out, w_up, w_down):
    return (w_glu.astype(BF16), w_out[:ATTN_WIDTH].astype(BF16), w_out[ATTN_WIDTH:].astype(BF16),
            w_up.astype(BF16), w_down.astype(BF16))


def _post(x, ao, y_tb, weights, b_glu, g_post, g_ffn_pre, g_ffn_post, tm):
    B, T, _ = x.shape
    wglu, woa, wos, wup, wdn = weights

    def row(w):
        return pl.BlockSpec((None, tm, w), lambda b, t: (b, t, 0))

    def vec(a):
        return a.reshape(1, -1)

    return pl.pallas_call(
        _post_kernel,
        grid=(B, T // tm),
        in_specs=[row(D_MODEL), row(ATTN_WIDTH), pl.BlockSpec((tm, SSM_WIDTH), lambda b, t: (t, b)),
                  _const_spec((SSM_WIDTH, SSM_WIDTH)), _const_spec((1, SSM_WIDTH)),
                  _const_spec((ATTN_WIDTH, D_MODEL)), _const_spec((SSM_WIDTH, D_MODEL)),
                  _const_spec((1, D_MODEL)), _const_spec((1, D_MODEL)),
                  _const_spec((D_MODEL, D_FF)), _const_spec((D_FF, D_MODEL)), _const_spec((1, D_MODEL))],
        out_specs=row(D_MODEL),
        out_shape=jax.ShapeDtypeStruct((B, T, D_MODEL), F32),
        compiler_params=_cparams(2),
        name="post",
    )(x, ao, y_tb, wglu, vec(b_glu), woa, wos, vec(g_post), vec(g_ffn_pre), wup, wdn, vec(g_ffn_post))


def _page_map(b, j, pt, *, r, n):
    return (pt[b, j * n + r], 0, 0)


def _skeys_kernel(pt_ref, qh_ref, ql_ref, w_ref, kn_ref, *rest, n_pages_step, dec_t):
    page_refs = rest[:n_pages_step]
    keys_ref, knew_ref = rest[n_pages_step:]
    j = pl.program_id(1)
    qh = qh_ref[...]
    ql = ql_ref[...]
    w = w_ref[...]

    def scores(kc):
        kh, kl = _split_bf16(kc)
        s = (jnp.dot(qh, kh, preferred_element_type=F32) + jnp.dot(ql, kh, preferred_element_type=F32)
             + jnp.dot(qh, kl, preferred_element_type=F32))
        sc = None
        for h in range(N_IDX_HEADS):
            t = jnp.maximum(s[h * dec_t:(h + 1) * dec_t], 0.0) * w[:, h:h + 1]
            sc = t if sc is None else sc + t
        return _sortable(sc)

    keys_ref[...] = scores(jnp.concatenate([r[...] for r in page_refs], axis=1))

    @pl.when(j == pl.num_programs(1) - 1)
    def _():
        kn = scores(kn_ref[...])
        qi_ = lax.broadcasted_iota(I32, kn.shape, 0)
        ki_ = lax.broadcasted_iota(I32, kn.shape, 1)
        knew_ref[...] = jnp.where(ki_ <= qi_, kn, INT_MIN)


def _ssearch_kernel(kp_ref, kn_ref, sel_ref, *, topk, past_len):
    def count(pred_past, pred_new):
        c = jnp.sum(jnp.where(pred_past, 1, 0).astype(I32), axis=1, keepdims=True)
        return c + jnp.sum(jnp.where(pred_new, 1, 0).astype(I32), axis=1, keepdims=True)

    def count_ge(cand):
        return count(kp_ref[...] >= cand, kn_ref[...] >= cand)

    rows = kp_ref.shape[0]
    tau = jnp.where(count_ge(jnp.zeros((rows, 1), I32)) >= topk, 0, INT_MIN).astype(I32)

    def bit_body(b, tau):
        cand = tau + (jnp.int32(1) << (30 - b))
        return jnp.where(count_ge(cand) >= topk, cand, tau)

    tau = lax.fori_loop(0, 31, bit_body, tau)
    tau = jnp.maximum(tau, INT_MIN + 1)
    quota = topk - count(kp_ref[...] > tau, kn_ref[...] > tau)
    pos_p = lax.broadcasted_iota(I32, kp_ref.shape, 1)
    pos_n = past_len + lax.broadcasted_iota(I32, kn_ref.shape, 1)
    n_bits = max(1, (past_len + kn_ref.shape[1] - 1).bit_length())

    def idx_body(b, lim):
        cand = lim + (jnp.int32(1) << (n_bits - 1 - b))
        ties = count((kp_ref[...] == tau) & (pos_p < cand), (kn_ref[...] == tau) & (pos_n < cand))
        return jnp.where(ties < quota, cand, lim)

    lim = lax.fori_loop(0, n_bits, idx_body, jnp.zeros_like(tau))
    kp = kp_ref[...]
    kn = kn_ref[...]
    sel_ref[:, :past_len] = jnp.where((kp > tau) | ((kp == tau) & (pos_p <= lim)), 1.0, 0.0)
    sel_ref[:, past_len:] = jnp.where((kn > tau) | ((kn == tau) & (pos_n <= lim)), 1.0, 0.0)


def _sample_select(page_table, qh, ql, w, ki_new_pad, cache_kidx, n_pages_step):
    DB, n_pages = page_table.shape
    dec_t = w.shape[1]
    past_len = n_pages * PAGE_SIZE
    lp = past_len + LANES
    topk = min(TOPK_MAX, (past_len + dec_t) // 4)
    rows = N_IDX_HEADS * dec_t
    step_keys = n_pages_step * PAGE_SIZE

    def fixed(shape):
        return pl.BlockSpec((None,) + shape, lambda b, j, pt: (b, 0, 0))

    pages = [pl.BlockSpec((None, IDX_DIM, PAGE_SIZE), functools.partial(_page_map, r=r, n=n_pages_step))
             for r in range(n_pages_step)]
    grid_spec = pltpu.PrefetchScalarGridSpec(
        num_scalar_prefetch=1,
        grid=(DB, n_pages // n_pages_step),
        in_specs=[fixed((rows, IDX_DIM)), fixed((rows, IDX_DIM)), fixed((dec_t, N_IDX_HEADS)),
                  fixed((IDX_DIM, LANES))] + pages,
        out_specs=(pl.BlockSpec((None, dec_t, step_keys), lambda b, j, pt: (b, 0, j)), fixed((dec_t, LANES))),
    )
    keys_past, keys_new = pl.pallas_call(
        functools.partial(_skeys_kernel, n_pages_step=n_pages_step, dec_t=dec_t),
        grid_spec=grid_spec,
        out_shape=(jax.ShapeDtypeStruct((DB, dec_t, past_len), I32), jax.ShapeDtypeStruct((DB, dec_t, LANES), I32)),
        compiler_params=_cparams(2),
        name="sample_keys",
    )(page_table, qh, ql, w, ki_new_pad, *([cache_kidx] * n_pages_step))
    sel = pl.pallas_call(
        functools.partial(_ssearch_kernel, topk=topk, past_len=past_len),
        out_shape=jax.ShapeDtypeStruct((DB * dec_t, lp), F32),
        compiler_params=pltpu.CompilerParams(vmem_limit_bytes=VMEM_LIMIT),
        name="sample_search",
    )(keys_past.reshape(DB * dec_t, past_len), keys_new.reshape(DB * dec_t, LANES))
    return sel.reshape(DB, dec_t, lp)


def _sattn_kernel(pt_ref, sel_ref, q_ref, kn_ref, vn_ref, *rest, n_pages_step, past_len, dec_t):
    k_refs = rest[:n_pages_step]
    v_refs = rest[n_pages_step:2 * n_pages_step]
    o_ref, m_ref, l_ref, acc_ref = rest[2 * n_pages_step:]
    j = pl.program_id(1)
    step_keys = n_pages_step * PAGE_SIZE
    q = q_ref[...]

    @pl.when(j == 0)
    def _():
        m_ref[...] = jnp.full(m_ref.shape, NEG, F32)
        l_ref[...] = jnp.zeros(l_ref.shape, F32)
        acc_ref[...] = jnp.zeros(acc_ref.shape, F32)

    def update(kc, vc, sel):
        n = kc.shape[1]
        lg = jnp.dot(q, kc, preferred_element_type=F32)
        lg = jnp.where(sel[None] > 0.5, lg.reshape(N_HEADS, dec_t, n), NEG).reshape(N_HEADS * dec_t, n)
        m_prev = m_ref[...]
        m_new = jnp.maximum(m_prev, jnp.max(lg, axis=1, keepdims=True))
        alpha = jnp.exp2(m_prev - m_new)
        p = jnp.exp2(lg - m_new)
        l_ref[...] = alpha * l_ref[...] + jnp.sum(p, axis=1, keepdims=True)
        acc_ref[...] = alpha * acc_ref[...] + lax.dot_general(p.astype(BF16), vc, NT_DIMS,
                                                              preferred_element_type=F32)
        m_ref[...] = m_new

    kc = jnp.concatenate([r[...] for r in k_refs], axis=1).astype(BF16)
    vc = jnp.concatenate([r[...] for r in v_refs], axis=1).astype(BF16)
    update(kc, vc, sel_ref[:, pl.ds(pl.multiple_of(j * step_keys, step_keys), step_keys)])

    @pl.when(j == pl.num_programs(1) - 1)
    def _():
        update(kn_ref[...], vn_ref[...], sel_ref[:, past_len:])
        o_ref[...] = acc_ref[...] / l_ref[...]


def _sample_attention(page_table, sel, q, k_new_pad, v_new_pad, cache_k, cache_v, n_pages_step):
    DB, n_pages = page_table.shape
    dec_t = sel.shape[1]
    past_len = n_pages * PAGE_SIZE
    lp = past_len + LANES
    rows = N_HEADS * dec_t

    def fixed(shape):
        return pl.BlockSpec((None,) + shape, lambda b, j, pt: (b, 0, 0))

    def pages():
        return [pl.BlockSpec((None, KV_WIDTH, PAGE_SIZE), functools.partial(_page_map, r=r, n=n_pages_step))
                for r in range(n_pages_step)]

    grid_spec = pltpu.PrefetchScalarGridSpec(
        num_scalar_prefetch=1,
        grid=(DB, n_pages // n_pages_step),
        in_specs=[fixed((dec_t, lp)), fixed((rows, KV_WIDTH)), fixed((KV_WIDTH, LANES)),
                  fixed((KV_WIDTH, LANES))] + pages() + pages(),
        out_specs=fixed((rows, KV_WIDTH)),
        scratch_shapes=[pltpu.VMEM((rows, 1), F32), pltpu.VMEM((rows, 1), F32),
                        pltpu.VMEM((rows, KV_WIDTH), F32)],
    )
    return pl.pallas_call(
        functools.partial(_sattn_kernel, n_pages_step=n_pages_step, past_len=past_len, dec_t=dec_t),
        grid_spec=grid_spec,
        out_shape=jax.ShapeDtypeStruct((DB, rows, KV_WIDTH), F32),
        compiler_params=_cparams(2),
        name="sample_attention",
    )(page_table, sel, q, k_new_pad, v_new_pad, *([cache_k] * n_pages_step), *([cache_v] * n_pages_step))


def _largest_tile(n, cap):
    t = min(n, cap)
    while n % t:
        t //= 2
    return t


INPROJ_ROWS = 512
ATTN_TILE = 256
SCAN_STEPS = 256
POST_ROWS = 512
SAMPLE_ROWS = 256
SAMPLE_PAGES_PER_STEP = 32


def _tile_plan(T, n_sample_rows, n_pages):
    return dict(inproj=_largest_tile(T, INPROJ_ROWS), attn=_largest_tile(T, ATTN_TILE),
                scan=_largest_tile(T, SCAN_STEPS), post=_largest_tile(T, POST_ROWS),
                sample=_largest_tile(n_sample_rows, SAMPLE_ROWS),
                pages=_largest_tile(n_pages, SAMPLE_PAGES_PER_STEP))


def _layer(hp, hs, cache_k, cache_v, cache_kidx, st_re, st_im, page_table, g_pre, w_in, ssm_params, d_skip,
           w_glu, b_glu, w_out, g_post, g_ffn_pre, w_up, w_down, g_ffn_post):
    B, T, _ = hp.shape
    DB, dec_t, _ = hs.shape
    n_pages = page_table.shape[1]
    past_len = n_pages * PAGE_SIZE
    n_s = DB * dec_t
    idx_scale = (IDX_DIM ** -0.5) * (N_IDX_HEADS ** -0.5)
    tiles = _tile_plan(T, n_s, n_pages)

    wm, wih, wil = _inproj_weights(w_in)
    s5w = _s5_weights(*ssm_params)
    postw = _post_weights(w_glu, w_out, w_up, w_down)

    pos_p = jnp.arange(T, dtype=I32)
    q, kft, kb, vft, vt, u_tb, kift, ki4, qi4, misc = _inproj(hp, pos_p, g_pre, wm, wih, wil, tiles["inproj"])
    wit = jnp.swapaxes(misc[:, :, IDX_DIM:IDX_DIM + N_IDX_HEADS], 1, 2) * idx_scale
    ao = _prompt_attention(qi4, wit, ki4, q, kb, vt, tiles["attn"])
    h0 = jnp.zeros((N_SLAB, B, LANES), F32)
    y_tb, hfin = _s5_scan(u_tb, B, B, h0, s5w, d_skip, tiles["scan"])
    yp = _post(hp, ao, y_tb, postw, b_glu, g_post, g_ffn_pre, g_ffn_post, tiles["post"])
    srp, sip = _slabs_to_state(hfin)

    def kv_rows(a, nb, t):
        return jnp.transpose(a.reshape(nb, N_KV_HEADS, HEAD_DIM, t), (0, 3, 1, 2))

    prompt_out = (yp, kv_rows(kft, B, T), kv_rows(vft, B, T), jnp.swapaxes(kift, 1, 2), srp, sip)

    pos_s = jnp.tile(past_len + jnp.arange(dec_t, dtype=I32), DB)
    q, kft, _, vft, _, u_s, kift, _, qi4, misc = _inproj(hs.reshape(1, n_s, D_MODEL), pos_s, g_pre, wm, wih, wil,
                                                        tiles["sample"])
    qi4 = jnp.transpose(qi4.reshape(N_IDX_HEADS, 4 * IDX_DIM, DB, dec_t), (2, 0, 3, 1))

    def heads_first(a):
        return a.reshape(DB, a.shape[1] * dec_t, a.shape[3])

    def per_batch(a):
        return jnp.swapaxes(a.reshape(a.shape[1], DB, dec_t), 0, 1)

    def pad_new(a, dtype):
        return jnp.pad(a, ((0, 0), (0, 0), (0, LANES - dec_t))).astype(dtype)

    k_s, v_s, ki_s = per_batch(kft), per_batch(vft), per_batch(kift)
    qh = heads_first(qi4[..., :IDX_DIM])
    ql = heads_first(qi4[..., 2 * IDX_DIM:3 * IDX_DIM])
    w_s = misc.reshape(DB, dec_t, LANES)[:, :, IDX_DIM:IDX_DIM + N_IDX_HEADS] * idx_scale

    n_phys = cache_k.shape[0]
    pool_ki = jnp.swapaxes(cache_kidx, 1, 2)
    pool_k = jnp.transpose(cache_k, (0, 2, 3, 1)).reshape(n_phys, KV_WIDTH, PAGE_SIZE)
    pool_v = jnp.transpose(cache_v, (0, 2, 3, 1)).reshape(n_phys, KV_WIDTH, PAGE_SIZE)
    n_step = tiles["pages"]
    sel = _sample_select(page_table, qh, ql, w_s, pad_new(ki_s, F32), pool_ki, n_step)
    qs = jnp.transpose(q.reshape(N_HEADS, LANES, DB, dec_t), (2, 0, 3, 1)).reshape(DB, N_HEADS * dec_t, LANES)
    ao_s = _sample_attention(page_table, sel, qs, pad_new(k_s, BF16), pad_new(v_s, BF16), pool_k, pool_v, n_step)
    ao_s = ao_s.reshape(DB, N_KV_HEADS, Q_PER_KV, dec_t, N_KV_HEADS, HEAD_DIM)
    ao_s = jnp.stack([ao_s[:, g, :, :, g] for g in range(N_KV_HEADS)], axis=1).reshape(DB, N_HEADS, dec_t, HEAD_DIM)
    ao_s = jnp.swapaxes(ao_s, 1, 2).reshape(1, n_s, ATTN_WIDTH).astype(BF16)
    u_t = jnp.swapaxes(u_s.reshape(DB, dec_t, SSM_WIDTH), 0, 1).reshape(n_s, SSM_WIDTH)
    y_t, hfin = _s5_scan(u_t, DB, 1, _state_to_slabs(st_re, st_im), s5w, d_skip, dec_t)
    y_s = jnp.swapaxes(y_t.reshape(dec_t, DB, SSM_WIDTH), 0, 1).reshape(n_s, SSM_WIDTH)
    ys = _post(hs.reshape(1, n_s, D_MODEL), ao_s, y_s, postw, b_glu, g_post, g_ffn_pre, g_ffn_post,
               tiles["sample"])
    srs, sis = _slabs_to_state(hfin)
    sample_out = (ys.reshape(DB, dec_t, D_MODEL), kv_rows(k_s, DB, dec_t), kv_rows(v_s, DB, dec_t),
                  jnp.swapaxes(ki_s, 1, 2), srs, sis)
    return prompt_out, sample_out


def kernel(x_prompt, x_sample, cache_k, cache_v, cache_kidx, state_ssm_re, state_ssm_im, page_table, norm_mix_pre, w_in, lambda_re, lambda_im, log_dt, b_re, b_im, c_re, c_im, d_skip, w_glu, b_glu, w_out, norm_mix_post, norm_ffn_pre, w_up, w_down, norm_ffn_post):
    depth = w_in.shape[0]
    hp, hs = x_prompt, x_sample
    p_acc = [[] for _ in range(5)]
    s_acc = [[] for _ in range(5)]
    for l in range(depth):
        ssm_params = (lambda_re[l], lambda_im[l], log_dt[l], b_re[l], b_im[l], c_re[l], c_im[l])
        p_out, s_out = _layer(hp, hs, cache_k[l], cache_v[l], cache_kidx[l], state_ssm_re[l], state_ssm_im[l],
                              page_table, norm_mix_pre[l], w_in[l], ssm_params, d_skip[l], w_glu[l], b_glu[l],
                              w_out[l], norm_mix_post[l], norm_ffn_pre[l], w_up[l], w_down[l], norm_ffn_post[l])
        hp, hs = p_out[0], s_out[0]
        for acc, vals in ((p_acc, p_out[1:]), (s_acc, s_out[1:])):
            for a, v in zip(acc, vals):
                a.append(v)
    return (hp, hs) + tuple(jnp.stack(a) for a in p_acc) + tuple(jnp.stack(a) for a in s_acc)
```
